```python
import jax, jax.numpy as jnp
from jax import lax
import numpy as np

D_MODEL = 1024
BATCH = 32
SEQ = 2048
DEPTH = 1

D_PLE = 256
RET_HEAD_DIM = 128
RET_WIDTH = D_MODEL // 2
RET_HEADS = RET_WIDTH // RET_HEAD_DIM
RET_CHUNK = 128
RWKV_HEAD_DIM = 64
RWKV_WIDTH = D_MODEL - RET_WIDTH
RWKV_HEADS = RWKV_WIDTH // RWKV_HEAD_DIM
DECAY_LORA = 64
AAA_LORA = 64
GATE_LORA = 128
D_FF = ((8 * D_MODEL + 3 * 256 - 1) // (3 * 256)) * 256
RET_COLS = 4 * RET_WIDTH
RWKV_COLS = 3 * RWKV_WIDTH + DECAY_LORA + AAA_LORA + GATE_LORA
IN_COLS = RET_COLS + RWKV_COLS
NORM_EPS = 1e-6
RET_GN_EPS = 1e-5
RWKV_GN_EPS = 64e-5

kernel_name = "hybrid_retention_rwkv7_parallel_heads"


def rms_norm(x, w):
    x32 = x.astype(jnp.float32)
    y = x32 * lax.rsqrt(jnp.mean(x32 * x32, axis=-1, keepdims=True) + NORM_EPS)
    return (y * w.astype(jnp.float32)).astype(x.dtype)


def head_norm(y, eps):
    mu = jnp.mean(y, axis=-1, keepdims=True)
    var = jnp.mean(jnp.square(y - mu), axis=-1, keepdims=True)
    return (y - mu) * lax.rsqrt(var + eps)


def rotate_every_two(t):
    t1 = t[..., 0::2]
    t2 = t[..., 1::2]
    return jnp.stack((-t2, t1), axis=-1).reshape(t.shape)


def retention(q, k, v):
    B, S, H, d = q.shape
    C = RET_CHUNK
    N = S // C
    pos = jnp.arange(S, dtype=jnp.float32)
    angle = 1.0 / (10000.0 ** jnp.linspace(0.0, 1.0, d // 2, dtype=jnp.float32))
    angle = jnp.repeat(angle, 2)
    phase = pos[:, None] * angle[None, :]
    sin = jnp.sin(phase)[:, None, :]
    cos = jnp.cos(phase)[:, None, :]
    q = q * cos + rotate_every_two(q) * sin
    k = (k * cos + rotate_every_two(k) * sin) * (d ** -0.5)
    log_g = jnp.log(1.0 - 2.0 ** (-5.0 - jnp.arange(H, dtype=jnp.float32)))

    def chunks(t):
        return t.reshape(B, N, C, H, t.shape[-1]).transpose(0, 3, 1, 2, 4)

    qc, kc, vc = chunks(q), chunks(k), chunks(v)
    idx = jnp.arange(C, dtype=jnp.float32)
    rel = idx[:, None] - idx[None, :]
    inner_decay = jnp.where(rel[None] >= 0,
                            jnp.exp(log_g[:, None, None] * jnp.maximum(rel, 0.0)[None]),
                            0.0)
    scores = jnp.einsum('bhncd,bhnmd->bhncm', qc, kc) * inner_decay[None, :, None]
    inner = jnp.einsum('bhncm,bhnme->bhnce', scores, vc)
    k_to_end = jnp.exp(log_g[:, None] * (C - 1 - idx)[None])
    chunk_state = jnp.einsum('bhnmd,bhnme->bhnde',
                             kc * k_to_end[None, :, None, :, None], vc)
    chunk_decay = jnp.exp(log_g * C)[None, :, None, None]

    def step(R, U):
        return R * chunk_decay + U, R

    _, R_prev = lax.scan(step, jnp.zeros((B, H, d, vc.shape[-1]), jnp.float32),
                         chunk_state.transpose(2, 0, 1, 3, 4))
    R_prev = R_prev.transpose(1, 2, 0, 3, 4)
    q_from_start = jnp.exp(log_g[:, None] * (idx + 1.0)[None])
    cross = jnp.einsum('bhncd,bhnde->bhnce', qc, R_prev) * q_from_start[None, :, None, :, None]
    return (inner + cross).transpose(0, 2, 3, 1, 4).reshape(B, S, H, vc.shape[-1])


def rwkv7_mix(h, mu, w0, w2, a0, a2, g2, k_k, k_a, r_k, ln_w, ln_b):
    B, S, _ = h.shape
    H, N, W = RWKV_HEADS, RWKV_HEAD_DIM, RWKV_WIDTH
    prev = jnp.pad(h, ((0, 0), (1, 0), (0, 0)))[:, :S]
    h = h + (prev - h) * mu
    r = h[..., :W]
    k = h[..., W:2 * W]
    v = h[..., 2 * W:3 * W]
    o = 3 * W
    wd = h[..., o:o + DECAY_LORA]
    o += DECAY_LORA
    ad = h[..., o:o + AAA_LORA]
    o += AAA_LORA
    gd = h[..., o:o + GATE_LORA]
    w_log = -jax.nn.softplus(-(w0 + jnp.tanh(wd) @ w2)) - 0.5
    decay = jnp.exp(-jnp.exp(w_log))
    a = jax.nn.sigmoid(a0 + ad @ a2)
    g = jax.nn.sigmoid(gd) @ g2

    def heads(t):
        return t.reshape(B, S, H, N)

    r, k, v, decay, a = heads(r), heads(k), heads(v), heads(decay), heads(a)
    kk = k * k_k.reshape(H, N)
    kk = kk * lax.rsqrt(jnp.maximum(jnp.sum(kk * kk, axis=-1, keepdims=True), 1e-24))
    k = k * (1.0 + (a - 1.0) * k_a.reshape(H, N))

    def step(state, inp):
        r_t, w_t, k_t, v_t, kk_t, b_t = inp
        sa = jnp.einsum('bhvk,bhk->bhv', state, -kk_t)
        state = (state * w_t[:, :, None, :] + sa[..., None] * b_t[:, :, None, :]
                 + v_t[..., None] * k_t[:, :, None, :])
        return state, jnp.einsum('bhvk,bhk->bhv', state, r_t)

    def tm(t):
        return jnp.swapaxes(t, 0, 1)

    _, y = lax.scan(step, jnp.zeros((B, H, N, N), jnp.float32),
                    (tm(r), tm(decay), tm(k), tm(v), tm(kk), tm(kk * a)))
    y = tm(y)
    y = head_norm(y, RWKV_GN_EPS).reshape(B, S, W) * ln_w + ln_b
    bonus = jnp.sum(r * k * r_k, axis=-1, keepdims=True) * v
    return (y + bonus.reshape(B, S, W)) * g


def setup_inputs(seed: int = 0) -> dict:
    key = jax.random.key(seed)
    ks = jax.random.split(key, 32)
    f32 = jnp.float32

    def nrm(k, shape, scale):
        return jax.random.normal(k, shape, f32) * scale

    def gain(k, shape):
        return 1.0 + 0.02 * jax.random.normal(k, shape, f32)

    L = DEPTH
    return {
        "x": nrm(ks[0], (BATCH, SEQ, D_MODEL), 1.0),
        "p": nrm(ks[1], (DEPTH, BATCH, SEQ, D_PLE), 1.0),
        "norm_mix_w": gain(ks[2], (L, D_MODEL)),
        "w_in": nrm(ks[3], (L, D_MODEL, IN_COLS), D_MODEL ** -0.5),
        "ret_norm_w": gain(ks[4], (L, RET_WIDTH)),
        "rw_mu": jax.random.uniform(ks[5], (L, RWKV_COLS), f32, 0.0, 1.0),
        "rw_w0": jax.random.uniform(ks[6], (L, RWKV_WIDTH), f32, -6.0, 0.0),
        "rw_w2": nrm(ks[7], (L, DECAY_LORA, RWKV_WIDTH), 0.1 * DECAY_LORA ** -0.5),
        "rw_a0": nrm(ks[8], (L, RWKV_WIDTH), 0.1),
        "rw_a2": nrm(ks[9], (L, AAA_LORA, RWKV_WIDTH), 0.1 * AAA_LORA ** -0.5),
        "rw_g2": nrm(ks[10], (L, GATE_LORA, RWKV_WIDTH), GATE_LORA ** -0.5),
        "rw_k_k": 0.85 + 0.02 * jax.random.normal(ks[11], (L, RWKV_WIDTH), f32),
        "rw_k_a": gain(ks[12], (L, RWKV_WIDTH)),
        "rw_r_k": nrm(ks[13], (L, RWKV_HEADS, RWKV_HEAD_DIM), 0.1),
        "rw_ln_w": gain(ks[14], (L, RWKV_WIDTH)),
        "rw_ln_b": nrm(ks[15], (L, RWKV_WIDTH), 0.01),
        "w_o": nrm(ks[16], (L, D_MODEL, D_MODEL), D_MODEL ** -0.5),
        "norm_ffn_w": gain(ks[17], (L, D_MODEL)),
        "w_gate": nrm(ks[18], (L, D_MODEL, D_FF), D_MODEL ** -0.5),
        "w_up": nrm(ks[19], (L, D_MODEL, D_FF), D_MODEL ** -0.5),
        "w_down": nrm(ks[20], (L, D_FF, D_MODEL), D_FF ** -0.5),
        "norm_ple_w": gain(ks[21], (L, D_MODEL)),
        "w_ple_gate": nrm(ks[22], (L, D_MODEL, D_MODEL), D_MODEL ** -0.5),
        "w_ple_up": nrm(ks[23], (L, D_PLE, D_MODEL), D_PLE ** -0.5),
        "final_norm_w": gain(ks[24], (D_MODEL,)),
    }


def reference(x, p, norm_mix_w, w_in, ret_norm_w, rw_mu, rw_w0, rw_w2, rw_a0, rw_a2,
              rw_g2, rw_k_k, rw_k_a, rw_r_k, rw_ln_w, rw_ln_b, w_o, norm_ffn_w,
              w_gate, w_up, w_down, norm_ple_w, w_ple_gate, w_ple_up, final_norm_w):
    B, S, _ = x.shape
    for i in range(DEPTH):
        hn = rms_norm(x, norm_mix_w[i])
        proj = (hn @ w_in[i]).astype(jnp.float32)
        q = proj[..., 0:RET_WIDTH].reshape(B, S, RET_HEADS, RET_HEAD_DIM)
        k = proj[..., RET_WIDTH:2 * RET_WIDTH].reshape(B, S, RET_HEADS, RET_HEAD_DIM)
        v = proj[..., 2 * RET_WIDTH:3 * RET_WIDTH].reshape(B, S, RET_HEADS, RET_HEAD_DIM)
        ret_gate = proj[..., 3 * RET_WIDTH:RET_COLS]
        y_ret = head_norm(retention(q, k, v), RET_GN_EPS).reshape(B, S, RET_WIDTH)
        y_ret = y_ret * ret_norm_w[i] * jax.nn.silu(ret_gate)
        y_rw = rwkv7_mix(proj[..., RET_COLS:], rw_mu[i], rw_w0[i], rw_w2[i], rw_a0[i],
                         rw_a2[i], rw_g2[i], rw_k_k[i], rw_k_a[i], rw_r_k[i],
                         rw_ln_w[i], rw_ln_b[i])
        mixed = jnp.concatenate([y_ret, y_rw], axis=-1).astype(x.dtype)
        x = x + mixed @ w_o[i]
        hf = rms_norm(x, norm_ffn_w[i])
        x = x + (jax.nn.silu(hf @ w_gate[i]) * (hf @ w_up[i])) @ w_down[i]
        hp = rms_norm(x, norm_ple_w[i])
        x = x + (p[i] @ w_ple_up[i]) * jax.nn.sigmoid(hp @ w_ple_gate[i])
    return rms_norm(x, final_norm_w)
```

```python
import functools

import jax
import jax.numpy as jnp
import numpy as np
from jax import lax
from jax.experimental import pallas as pl
from jax.experimental.pallas import tpu as pltpu

F32 = jnp.float32
BF16 = jnp.bfloat16

D_MODEL = 1024
D_PLE = 256
RET_HEAD_DIM = 128
RET_WIDTH = 512
RET_HEADS = 4
RET_CHUNK = 128
RET_COLS = 4 * RET_WIDTH
RWKV_HEAD_DIM = 64
RWKV_WIDTH = 512
RWKV_HEADS = 8
DECAY_LORA = 64
AAA_LORA = 64
GATE_LORA = 128
RWKV_COLS = 3 * RWKV_WIDTH + DECAY_LORA + AAA_LORA + GATE_LORA
D_FF = 2816
NORM_EPS = 1e-6
RET_GN_EPS = 1e-5
RWKV_GN_EPS = 64e-5

RWKV_CHUNK = 64
TOKEN_TILE = 256
V7X_VMEM_LIMIT = 56 * 1024 * 1024


def _dot(a, b):
    return jnp.dot(a, b, preferred_element_type=F32)


def _dot_nt(a, b):
    return lax.dot_general(a, b, (((1,), (1,)), ((), ())), preferred_element_type=F32)


def _dot_tn(a, b):
    return lax.dot_general(a, b, (((0,), (0,)), ((), ())), preferred_element_type=F32)


def _rms(x, w):
    ms = jnp.mean(x * x, axis=-1, keepdims=True)
    return x * lax.rsqrt(ms + NORM_EPS) * w


def _sigmoid(x):
    return 1.0 / (1.0 + jnp.exp(-x))


def _const_spec(shape):
    nd = len(shape)
    return pl.BlockSpec(shape, lambda *_: (0,) * nd, pipeline_mode=pl.Buffered(1))


def _in_proj_kernel(x_ref, nw_ref, wret_ref, wrw_ref, oret_ref, orw_ref):
    hn = _rms(x_ref[...], nw_ref[...]).astype(BF16)
    oret_ref[...] = _dot(hn, wret_ref[...])
    orw_ref[...] = _dot(hn, wrw_ref[...])


def _in_proj(x2, nw, w_ret, w_rw):
    m = x2.shape[0]
    tm = TOKEN_TILE
    return pl.pallas_call(
        _in_proj_kernel,
        grid=(m // tm,),
        in_specs=[
            pl.BlockSpec((tm, D_MODEL), lambda i: (i, 0)),
            _const_spec((1, D_MODEL)),
            _const_spec((D_MODEL, RET_COLS)),
            _const_spec((D_MODEL, RWKV_COLS)),
        ],
        out_specs=[
            pl.BlockSpec((tm, RET_COLS), lambda i: (i, 0)),
            pl.BlockSpec((tm, RWKV_COLS), lambda i: (i, 0)),
        ],
        out_shape=[
            jax.ShapeDtypeStruct((m, RET_COLS), F32),
            jax.ShapeDtypeStruct((m, RWKV_COLS), F32),
        ],
        compiler_params=pltpu.CompilerParams(
            dimension_semantics=("arbitrary",), vmem_limit_bytes=V7X_VMEM_LIMIT),
        name="in_proj",
    )(x2, nw, w_ret, w_rw)


def _retention_kernel(p_ref, cos_ref, sine_ref, sino_ref, mask_ref, kte_ref,
                      qfs_ref, cd_ref, nw_ref, o_ref, state_ref):
    @pl.when(pl.program_id(1) == 0)
    def _():
        state_ref[...] = jnp.zeros_like(state_ref)

    cos = cos_ref[...]
    sine = sine_ref[...]
    sino = sino_ref[...]

    def rot(t):
        return (t * cos + pltpu.roll(t, RET_HEAD_DIM - 1, 1) * sine
                + pltpu.roll(t, 1, 1) * sino)

    for h in range(RET_HEADS):
        lo = h * RET_HEAD_DIM
        hi = lo + RET_HEAD_DIM
        q = rot(p_ref[0, :, lo:hi])
        k = rot(p_ref[0, :, RET_WIDTH + lo:RET_WIDTH + hi])
        v = p_ref[0, :, 2 * RET_WIDTH + lo:2 * RET_WIDTH + hi]
        g = p_ref[0, :, 3 * RET_WIDTH + lo:3 * RET_WIDTH + hi]
        qb = q.astype(BF16)
        kb = k.astype(BF16)
        vb = v.astype(BF16)
        r_prev = state_ref[h]
        scores = _dot_nt(qb, kb) * mask_ref[h]
        inner = _dot(scores.astype(BF16), vb)
        cross = _dot(qb, r_prev.astype(BF16)) * qfs_ref[h]
        kd = (k * kte_ref[h]).astype(BF16)
        state_ref[h] = r_prev * cd_ref[h] + _dot_tn(kd, vb)
        y = inner + cross
        mu = jnp.mean(y, axis=-1, keepdims=True)
        d = y - mu
        var = jnp.mean(d * d, axis=-1, keepdims=True)
        yn = d * lax.rsqrt(var + RET_GN_EPS)
        o_ref[0, :, lo:hi] = yn * nw_ref[:, lo:hi] * (g * _sigmoid(g))


def _retention_tables(seq):
    d, c, hh = RET_HEAD_DIM, RET_CHUNK, RET_HEADS
    pos = jnp.arange(seq, dtype=F32)
    angle = 1.0 / (10000.0 ** jnp.linspace(0.0, 1.0, d // 2, dtype=F32))
    angle = jnp.repeat(angle, 2)
    phase = pos[:, None] * angle[None, :]
    sin = jnp.sin(phase)
    cos = jnp.cos(phase)
    even = (jnp.arange(d) % 2 == 0)[None, :]
    sin_e = jnp.where(even, -sin, 0.0)
    sin_o = jnp.where(even, 0.0, sin)
    log_g = jnp.log(1.0 - 2.0 ** (-5.0 - jnp.arange(hh, dtype=F32)))
    idx = jnp.arange(c, dtype=F32)
    rel = idx[:, None] - idx[None, :]
    scale = d ** -0.5
    mask = jnp.where(rel[None] >= 0,
                     jnp.exp(log_g[:, None, None] * jnp.maximum(rel, 0.0)[None]), 0.0) * scale
    kte = jnp.exp(log_g[:, None] * (c - 1 - idx)[None]) * scale
    kte = jnp.broadcast_to(kte[:, :, None], (hh, c, d))
    qfs = jnp.exp(log_g[:, None] * (idx + 1.0)[None])
    qfs = jnp.broadcast_to(qfs[:, :, None], (hh, c, d))
    cd = jnp.broadcast_to(jnp.exp(log_g * c)[:, None, None], (hh, 1, d))
    return cos, sin_e, sin_o, mask, kte, qfs, cd


def _retention(proj_ret, ret_norm_w):
    b, s, _ = proj_ret.shape
    c, d, hh = RET_CHUNK, RET_HEAD_DIM, RET_HEADS
    cos, sin_e, sin_o, mask, kte, qfs, cd = _retention_tables(s)
    tab_spec = pl.BlockSpec((c, d), lambda i, j: (j, 0))
    return pl.pallas_call(
        _retention_kernel,
        grid=(b, s // c),
        in_specs=[
            pl.BlockSpec((1, c, RET_COLS), lambda i, j: (i, j, 0)),
            tab_spec, tab_spec, tab_spec,
            _const_spec((hh, c, c)),
            _const_spec((hh, c, d)),
            _const_spec((hh, c, d)),
            _const_spec((hh, 1, d)),
            _const_spec((1, RET_WIDTH)),
        ],
        out_specs=pl.BlockSpec((1, c, RET_WIDTH), lambda i, j: (i, j, 0)),
        out_shape=jax.ShapeDtypeStruct((b, s, RET_WIDTH), F32),
        scratch_shapes=[pltpu.VMEM((hh, d, d), F32)],
        compiler_params=pltpu.CompilerParams(
            dimension_semantics=("arbitrary", "arbitrary"),
            vmem_limit_bytes=V7X_VMEM_LIMIT),
        name="retention",
    )(proj_ret, cos, sin_e, sin_o, mask, kte, qfs, cd, ret_norm_w)


def _split3(x):
    hi = x.astype(BF16)
    r1 = x - hi.astype(F32)
    mid = r1.astype(BF16)
    lo = (r1 - mid.astype(F32)).astype(BF16)
    return hi, mid, lo


def _rwkv_kernel(p_ref, mu_ref, w0_ref, a0_ref, wwa_ref, g2_ref, kk_ref, ka_ref,
                 rk_ref, lnw_ref, lnb_ref, gsum_ref, o_ref, state_ref, carry_ref,
                 y_ref):
    t = RWKV_CHUNK
    n = RWKV_HEAD_DIM
    w = RWKV_WIDTH

    @pl.when(pl.program_id(1) == 0)
    def _():
        state_ref[...] = jnp.zeros_like(state_ref)
        carry_ref[...] = jnp.zeros_like(carry_ref)

    h = p_ref[0]
    row = lax.broadcasted_iota(jnp.int32, h.shape, 0)
    prev = jnp.where(row == 0, carry_ref[...], pltpu.roll(h, 1, 0))
    carry_ref[...] = h[t - 1:t, :]
    hs = h + (prev - h) * mu_ref[...]

    r = hs[:, 0:w]
    k = hs[:, w:2 * w]
    v = hs[:, 2 * w:3 * w]
    wa = hs[:, 3 * w:3 * w + DECAY_LORA + AAA_LORA]
    gd = hs[:, 3 * w + DECAY_LORA + AAA_LORA:]
    lane = lax.broadcasted_iota(jnp.int32, wa.shape, 1)
    wa = jnp.where(lane < DECAY_LORA, jnp.tanh(wa), wa)
    lora = _dot(wa.astype(BF16), wwa_ref[...])
    wp = w0_ref[...] + lora[:, :w]
    softplus = jnp.maximum(-wp, 0.0) + jnp.log(1.0 + jnp.exp(-jnp.abs(wp)))
    logw = -jnp.exp(-softplus - 0.5)
    a = _sigmoid(a0_ref[...] + lora[:, w:])
    g = _dot(_sigmoid(gd).astype(BF16), g2_ref[...])

    gmat = gsum_ref[...]

    def gsum(x):
        hi, mid, _ = _split3(x)
        return _dot(hi, gmat) + _dot(mid, gmat)

    kk = k * kk_ref[...]
    kk = kk * lax.rsqrt(jnp.maximum(gsum(kk * kk), 1e-24))
    k2 = k * (1.0 + (a - 1.0) * ka_ref[...])
    bonus = gsum(r * k2 * rk_ref[...]) * v
    bb = kk * a

    ri = lax.broadcasted_iota(jnp.int32, (t, t), 0)
    ci = lax.broadcasted_iota(jnp.int32, (t, t), 1)
    incl = ri >= ci
    strict = ri > ci
    tri = jnp.where(incl, 1.0, 0.0).astype(BF16)
    l_hi, l_mid, l_lo = _split3(logw)
    cum = _dot(tri, l_hi) + _dot(tri, l_mid) + _dot(tri, l_lo)
    cum_t = cum[t - 1:t, :]
    e_neg = jnp.exp(-cum)
    a_t = (-kk * jnp.exp(cum - logw)).astype(BF16)
    r_t = (r * jnp.exp(cum)).astype(BF16)
    b_h = (bb * e_neg).astype(BF16)
    k_h = (k2 * e_neg).astype(BF16)
    to_end = jnp.exp(cum_t - cum)
    b_w = (bb * to_end).astype(BF16)
    k_w = (k2 * to_end).astype(BF16)
    w_t = jnp.exp(cum_t)
    vb = v.astype(BF16)
    eye = jnp.where(ri == ci, 1.0, 0.0)

    for hd in range(RWKV_HEADS):
        sl = slice(hd * n, (hd + 1) * n)
        ah, rh, bh, kh, vh = a_t[:, sl], r_t[:, sl], b_h[:, sl], k_h[:, sl], vb[:, sl]
        s0 = state_ref[hd]
        s0b = s0.astype(BF16)
        l_ab = jnp.where(strict, _dot_nt(ah, bh), 0.0)
        l_ak = jnp.where(strict, _dot_nt(ah, kh), 0.0)
        m_rb = jnp.where(incl, _dot_nt(rh, bh), 0.0)
        m_rk = jnp.where(incl, _dot_nt(rh, kh), 0.0)
        rhs = _dot_nt(ah, s0b) + _dot(l_ak.astype(BF16), vh)
        x = eye + l_ab
        p = l_ab
        steps = int(np.log2(t)) - 1
        for _ in range(steps):
            pb = p.astype(BF16)
            p = _dot(pb, pb)
            x = x + _dot(x.astype(BF16), p.astype(BF16))
        u = _dot(x.astype(BF16), rhs.astype(BF16))
        ub = u.astype(BF16)
        y_ref[:, sl] = (_dot_nt(rh, s0b) + _dot(m_rb.astype(BF16), ub)
                        + _dot(m_rk.astype(BF16), vh))
        state_ref[hd] = (s0 * w_t[:, sl] + _dot_tn(ub, b_w[:, sl])
                         + _dot_tn(vh, k_w[:, sl]))

    y = y_ref[...]
    mu = gsum(y) * (1.0 / n)
    d = y - mu
    var = gsum(d * d) * (1.0 / n)
    yn = d * lax.rsqrt(var + RWKV_GN_EPS) * lnw_ref[...] + lnb_ref[...]
    o_ref[0] = (yn + bonus) * g


def _rwkv(proj_rw, mu, w0, a0, wwa, g2, k_k, k_a, r_k, ln_w, ln_b):
    b, s, _ = proj_rw.shape
    t, n, w, hh = RWKV_CHUNK, RWKV_HEAD_DIM, RWKV_WIDTH, RWKV_HEADS
    grp = jnp.arange(w) // n
    gmat = (grp[:, None] == grp[None, :]).astype(BF16)
    vec = _const_spec((1, w))
    return pl.pallas_call(
        _rwkv_kernel,
        grid=(b, s // t),
        in_specs=[
            pl.BlockSpec((1, t, RWKV_COLS), lambda i, j: (i, j, 0)),
            _const_spec((1, RWKV_COLS)),
            vec, vec,
            _const_spec((DECAY_LORA + AAA_LORA, 2 * w)),
            _const_spec((GATE_LORA, w)),
            vec, vec, vec, vec, vec,
            _const_spec((w, w)),
        ],
        out_specs=pl.BlockSpec((1, t, w), lambda i, j: (i, j, 0)),
        out_shape=jax.ShapeDtypeStruct((b, s, w), F32),
        scratch_shapes=[
            pltpu.VMEM((hh, n, n), F32),
            pltpu.VMEM((1, RWKV_COLS), F32),
            pltpu.VMEM((t, w), F32),
        ],
        compiler_params=pltpu.CompilerParams(
            dimension_semantics=("arbitrary", "arbitrary"),
            vmem_limit_bytes=V7X_VMEM_LIMIT),
        name="rwkv7",
    )(proj_rw, mu, w0, a0, wwa, g2, k_k, k_a, r_k, ln_w, ln_b, gmat)


def _tail_kernel(x_ref, yret_ref, yrw_ref, p_ref, wo_ref, nfw_ref, wg_ref, wu_ref,
                 wd_ref, npw_ref, wpg_ref, wpu_ref, fnw_ref, o_ref):
    x = x_ref[...]
    x = (x + _dot(yret_ref[...].astype(BF16), wo_ref[0:RET_WIDTH, :])
         + _dot(yrw_ref[...].astype(BF16), wo_ref[RET_WIDTH:, :]))
    hf = _rms(x, nfw_ref[...]).astype(BF16)
    gate = _dot(hf, wg_ref[...])
    up = _dot(hf, wu_ref[...])
    act = (gate * _sigmoid(gate) * up).astype(BF16)
    x = x + _dot(act, wd_ref[...])
    hp = _rms(x, npw_ref[...]).astype(BF16)
    pg = _sigmoid(_dot(hp, wpg_ref[...]))
    x = x + _dot(p_ref[...].astype(BF16), wpu_ref[...]) * pg
    o_ref[...] = _rms(x, fnw_ref[...])


def _tail(x2, y_ret, y_rw, p2, w_o, nfw, w_gate, w_up, w_down, npw, w_pg, w_pu, fnw):
    m = x2.shape[0]
    tm = TOKEN_TILE
    vec = _const_spec((1, D_MODEL))
    return pl.pallas_call(
        _tail_kernel,
        grid=(m // tm,),
        in_specs=[
            pl.BlockSpec((tm, D_MODEL), lambda i: (i, 0)),
            pl.BlockSpec((tm, RET_WIDTH), lambda i: (i, 0)),
            pl.BlockSpec((tm, RWKV_WIDTH), lambda i: (i, 0)),
            pl.BlockSpec((tm, D_PLE), lambda i: (i, 0)),
            _const_spec((D_MODEL, D_MODEL)),
            vec,
            _const_spec((D_MODEL, D_FF)),
            _const_spec((D_MODEL, D_FF)),
            _const_spec((D_FF, D_MODEL)),
            vec,
            _const_spec((D_MODEL, D_MODEL)),
            _const_spec((D_PLE, D_MODEL)),
            vec,
        ],
        out_specs=pl.BlockSpec((tm, D_MODEL), lambda i: (i, 0)),
        out_shape=jax.ShapeDtypeStruct((m, D_MODEL), F32),
        compiler_params=pltpu.CompilerParams(
            dimension_semantics=("arbitrary",), vmem_limit_bytes=V7X_VMEM_LIMIT),
        name="tail",
    )(x2, y_ret, y_rw, p2, w_o, nfw, w_gate, w_up, w_down, npw, w_pg, w_pu, fnw)


def kernel(x, p, norm_mix_w, w_in, ret_norm_w, rw_mu, rw_w0, rw_w2, rw_a0, rw_a2,
           rw_g2, rw_k_k, rw_k_a, rw_r_k, rw_ln_w, rw_ln_b, w_o, norm_ffn_w,
           w_gate, w_up, w_down, norm_ple_w, w_ple_gate, w_ple_up, final_norm_w):
    b, s, dm = x.shape
    assert w_in.shape[0] == 1, "single-layer trunk only"
    x2 = x.reshape(b * s, dm)
    row = lambda a: a.reshape(1, -1)
    for i in range(1):
        w_in_b = w_in[i].astype(BF16)
        proj_ret, proj_rw = _in_proj(x2, row(norm_mix_w[i]),
                                     w_in_b[:, :RET_COLS], w_in_b[:, RET_COLS:])
        y_ret = _retention(proj_ret.reshape(b, s, RET_COLS), row(ret_norm_w[i]))
        zeros = jnp.zeros((DECAY_LORA, RWKV_WIDTH), F32)
        wwa = jnp.concatenate([
            jnp.concatenate([rw_w2[i], zeros], axis=1),
            jnp.concatenate([zeros, rw_a2[i]], axis=1)], axis=0).astype(BF16)
        y_rw = _rwkv(proj_rw.reshape(b, s, RWKV_COLS), row(rw_mu[i]), row(rw_w0[i]),
                     row(rw_a0[i]), wwa, rw_g2[i].astype(BF16), row(rw_k_k[i]),
                     row(rw_k_a[i]), row(rw_r_k[i]), row(rw_ln_w[i]), row(rw_ln_b[i]))
        fnw = final_norm_w
        x2 = _tail(x2, y_ret.reshape(b * s, RET_WIDTH), y_rw.reshape(b * s, RWKV_WIDTH),
                   p[i].reshape(b * s, D_PLE), w_o[i].astype(BF16), row(norm_ffn_w[i]),
                   w_gate[i].astype(BF16), w_up[i].astype(BF16), w_down[i].astype(BF16),
                   row(norm_ple_w[i]), w_ple_gate[i].astype(BF16),
                   w_ple_up[i].astype(BF16), row(fnw))
    return x2.reshape(b, s, dm)
```

```python
import functools

import jax
import jax.numpy as jnp
import numpy as np
from jax import lax
from jax.experimental import pallas as pl
from jax.experimental.pallas import tpu as pltpu

F32 = jnp.float32
BF16 = jnp.bfloat16

D_MODEL = 1024
D_PLE = 256
RET_HEAD_DIM = 128
RET_WIDTH = 512
RET_HEADS = 4
RET_CHUNK = 128
RET_COLS = 4 * RET_WIDTH
RWKV_HEAD_DIM = 64
RWKV_WIDTH = 512
RWKV_HEADS = 8
DECAY_LORA = 64
AAA_LORA = 64
GATE_LORA = 128
RWKV_COLS = 3 * RWKV_WIDTH + DECAY_LORA + AAA_LORA + GATE_LORA
D_FF = 2816
NORM_EPS = 1e-6
RET_GN_EPS = 1e-5
RWKV_GN_EPS = 64e-5

RWKV_CHUNK = 64
RWKV_BLOCK = 256
TOKEN_TILE = 256
V7X_VMEM_LIMIT = 56 * 1024 * 1024


def _dot(a, b):
    return jnp.dot(a, b, preferred_element_type=F32)


def _dot_nt(a, b):
    return lax.dot_general(a, b, (((1,), (1,)), ((), ())), preferred_element_type=F32)


def _dot_tn(a, b):
    return lax.dot_general(a, b, (((0,), (0,)), ((), ())), preferred_element_type=F32)


def _rms(x, w):
    ms = jnp.mean(x * x, axis=-1, keepdims=True)
    return x * lax.rsqrt(ms + NORM_EPS) * w


def _sigmoid(x):
    return 1.0 / (1.0 + jnp.exp(-x))


def _const_spec(shape):
    nd = len(shape)
    return pl.BlockSpec(shape, lambda *_: (0,) * nd, pipeline_mode=pl.Buffered(1))


def _in_proj_kernel(x_ref, nw_ref, wret_ref, wrw_ref, oret_ref, orw_ref):
    hn = _rms(x_ref[...], nw_ref[...]).astype(BF16)
    oret_ref[...] = _dot(hn, wret_ref[...])
    orw_ref[...] = _dot(hn, wrw_ref[...])


def _in_proj(x2, nw, w_ret, w_rw):
    m = x2.shape[0]
    tm = TOKEN_TILE
    return pl.pallas_call(
        _in_proj_kernel,
        grid=(m // tm,),
        in_specs=[
            pl.BlockSpec((tm, D_MODEL), lambda i: (i, 0)),
            _const_spec((1, D_MODEL)),
            _const_spec((D_MODEL, RET_COLS)),
            _const_spec((D_MODEL, RWKV_COLS)),
        ],
        out_specs=[
            pl.BlockSpec((tm, RET_COLS), lambda i: (i, 0)),
            pl.BlockSpec((tm, RWKV_COLS), lambda i: (i, 0)),
        ],
        out_shape=[
            jax.ShapeDtypeStruct((m, RET_COLS), F32),
            jax.ShapeDtypeStruct((m, RWKV_COLS), F32),
        ],
        compiler_params=pltpu.CompilerParams(
            dimension_semantics=("arbitrary",), vmem_limit_bytes=V7X_VMEM_LIMIT),
        name="in_proj",
    )(x2, nw, w_ret, w_rw)


def _retention_kernel(p_ref, cos_ref, sine_ref, sino_ref, mask_ref, kte_ref,
                      qfs_ref, cd_ref, nw_ref, o_ref, state_ref):
    @pl.when(pl.program_id(1) == 0)
    def _():
        state_ref[...] = jnp.zeros_like(state_ref)

    cos = cos_ref[...]
    sine = sine_ref[...]
    sino = sino_ref[...]

    def rot(t):
        return (t * cos + pltpu.roll(t, RET_HEAD_DIM - 1, 1) * sine
                + pltpu.roll(t, 1, 1) * sino)

    for h in range(RET_HEADS):
        lo = h * RET_HEAD_DIM
        hi = lo + RET_HEAD_DIM
        q = rot(p_ref[0, :, lo:hi])
        k = rot(p_ref[0, :, RET_WIDTH + lo:RET_WIDTH + hi])
        v = p_ref[0, :, 2 * RET_WIDTH + lo:2 * RET_WIDTH + hi]
        g = p_ref[0, :, 3 * RET_WIDTH + lo:3 * RET_WIDTH + hi]
        qb = q.astype(BF16)
        kb = k.astype(BF16)
        vb = v.astype(BF16)
        r_prev = state_ref[h]
        scores = _dot_nt(qb, kb) * mask_ref[h]
        inner = _dot(scores.astype(BF16), vb)
        cross = _dot(qb, r_prev.astype(BF16)) * qfs_ref[h]
        kd = (k * kte_ref[h]).astype(BF16)
        state_ref[h] = r_prev * cd_ref[h] + _dot_tn(kd, vb)
        y = inner + cross
        mu = jnp.mean(y, axis=-1, keepdims=True)
        d = y - mu
        var = jnp.mean(d * d, axis=-1, keepdims=True)
        yn = d * lax.rsqrt(var + RET_GN_EPS)
        o_ref[0, :, lo:hi] = yn * nw_ref[:, lo:hi] * (g * _sigmoid(g))


def _retention_tables(seq):
    d, c, hh = RET_HEAD_DIM, RET_CHUNK, RET_HEADS
    pos = jnp.arange(seq, dtype=F32)
    angle = 1.0 / (10000.0 ** jnp.linspace(0.0, 1.0, d // 2, dtype=F32))
    angle = jnp.repeat(angle, 2)
    phase = pos[:, None] * angle[None, :]
    sin = jnp.sin(phase)
    cos = jnp.cos(phase)
    even = (jnp.arange(d) % 2 == 0)[None, :]
    sin_e = jnp.where(even, -sin, 0.0)
    sin_o = jnp.where(even, 0.0, sin)
    log_g = jnp.log(1.0 - 2.0 ** (-5.0 - jnp.arange(hh, dtype=F32)))
    idx = jnp.arange(c, dtype=F32)
    rel = idx[:, None] - idx[None, :]
    scale = d ** -0.5
    mask = jnp.where(rel[None] >= 0,
                     jnp.exp(log_g[:, None, None] * jnp.maximum(rel, 0.0)[None]), 0.0) * scale
    kte = jnp.exp(log_g[:, None] * (c - 1 - idx)[None]) * scale
    kte = jnp.broadcast_to(kte[:, :, None], (hh, c, d))
    qfs = jnp.exp(log_g[:, None] * (idx + 1.0)[None])
    qfs = jnp.broadcast_to(qfs[:, :, None], (hh, c, d))
    cd = jnp.broadcast_to(jnp.exp(log_g * c)[:, None, None], (hh, 1, d))
    return cos, sin_e, sin_o, mask, kte, qfs, cd


def _retention(proj_ret, ret_norm_w):
    b, s, _ = proj_ret.shape
    c, d, hh = RET_CHUNK, RET_HEAD_DIM, RET_HEADS
    cos, sin_e, sin_o, mask, kte, qfs, cd = _retention_tables(s)
    tab_spec = pl.BlockSpec((c, d), lambda i, j: (j, 0))
    return pl.pallas_call(
        _retention_kernel,
        grid=(b, s // c),
        in_specs=[
            pl.BlockSpec((1, c, RET_COLS), lambda i, j: (i, j, 0)),
            tab_spec, tab_spec, tab_spec,
            _const_spec((hh, c, c)),
            _const_spec((hh, c, d)),
            _const_spec((hh, c, d)),
            _const_spec((hh, 1, d)),
            _const_spec((1, RET_WIDTH)),
        ],
        out_specs=pl.BlockSpec((1, c, RET_WIDTH), lambda i, j: (i, j, 0)),
        out_shape=jax.ShapeDtypeStruct((b, s, RET_WIDTH), F32),
        scratch_shapes=[pltpu.VMEM((hh, d, d), F32)],
        compiler_params=pltpu.CompilerParams(
            dimension_semantics=("arbitrary", "arbitrary"),
            vmem_limit_bytes=V7X_VMEM_LIMIT),
        name="retention",
    )(proj_ret, cos, sin_e, sin_o, mask, kte, qfs, cd, ret_norm_w)


def _split3(x):
    hi = x.astype(BF16)
    r1 = x - hi.astype(F32)
    mid = r1.astype(BF16)
    lo = (r1 - mid.astype(F32)).astype(BF16)
    return hi, mid, lo


def _rwkv_kernel(p_ref, mu_ref, w0_ref, a0_ref, wwa_ref, g2_ref, kk_ref, ka_ref,
                 rk_ref, lnw_ref, lnb_ref, gsum_ref, tri_ref, o_ref, state_ref,
                 carry_ref, y_ref):
    t = RWKV_CHUNK
    n = RWKV_HEAD_DIM
    w = RWKV_WIDTH
    tb = RWKV_BLOCK
    pw = 2 * n

    @pl.when(pl.program_id(1) == 0)
    def _():
        state_ref[...] = jnp.zeros_like(state_ref)
        carry_ref[...] = jnp.zeros_like(carry_ref)

    h = p_ref[0]
    row = lax.broadcasted_iota(jnp.int32, h.shape, 0)
    prev = jnp.where(row == 0, carry_ref[...], pltpu.roll(h, 1, 0))
    carry_ref[...] = h[tb - 1:tb, :]
    hs = h + (prev - h) * mu_ref[...]

    r = hs[:, 0:w]
    k = hs[:, w:2 * w]
    v = hs[:, 2 * w:3 * w]
    wa = hs[:, 3 * w:3 * w + DECAY_LORA + AAA_LORA]
    gd = hs[:, 3 * w + DECAY_LORA + AAA_LORA:]
    lane = lax.broadcasted_iota(jnp.int32, wa.shape, 1)
    wa = jnp.where(lane < DECAY_LORA, jnp.tanh(wa), wa)
    lora = _dot(wa.astype(BF16), wwa_ref[...])
    wp = w0_ref[...] + lora[:, :w]
    softplus = jnp.maximum(-wp, 0.0) + jnp.log(1.0 + jnp.exp(-jnp.abs(wp)))
    logw = -jnp.exp(-softplus - 0.5)
    a = _sigmoid(a0_ref[...] + lora[:, w:])
    g = _dot(_sigmoid(gd).astype(BF16), g2_ref[...])

    gmat = gsum_ref[...]

    def gsum(x):
        hi, mid, _ = _split3(x)
        return _dot(hi, gmat) + _dot(mid, gmat)

    kk = k * kk_ref[...]
    kk = kk * lax.rsqrt(jnp.maximum(gsum(kk * kk), 1e-24))
    k2 = k * (1.0 + (a - 1.0) * ka_ref[...])
    bonus = gsum(r * k2 * rk_ref[...]) * v
    bb = kk * a

    tri = tri_ref[...]
    l_hi, l_mid, l_lo = _split3(logw)
    cum = _dot(tri, l_hi) + _dot(tri, l_mid) + _dot(tri, l_lo)
    cum_end = jnp.concatenate(
        [jnp.broadcast_to(cum[(c + 1) * t - 1:(c + 1) * t, :], (t, w))
         for c in range(tb // t)], axis=0)
    e_neg = jnp.exp(-cum)
    a_t = (-kk * jnp.exp(cum - logw)).astype(BF16)
    r_t = (r * jnp.exp(cum)).astype(BF16)
    b_h = bb * e_neg
    k_h = k2 * e_neg
    to_end = jnp.exp(cum_end - cum)
    b_w = (bb * to_end).astype(BF16)
    k_w = (k2 * to_end).astype(BF16)
    w_end = jnp.exp(cum_end)

    ri = lax.broadcasted_iota(jnp.int32, (t, pw), 0)
    ci = lax.broadcasted_iota(jnp.int32, (t, pw), 1)
    cm = jnp.bitwise_and(ci, t - 1)
    incl = ri >= cm
    strict = ri > cm
    eye = jnp.where(ri == cm, 1.0, 0.0)
    lane0 = ci < n
    si = lax.broadcasted_iota(jnp.int32, (pw, pw), 0)
    sj = lax.broadcasted_iota(jnp.int32, (pw, pw), 1)
    same_head = (si < n) == (sj < n)

    def bd(x):
        return jnp.concatenate(
            [jnp.where(lane0, x, 0.0), jnp.where(lane0, 0.0, x)], axis=0).astype(BF16)

    nq = RWKV_HEADS // 2
    tiles = [(c, q) for c in range(tb // t) for q in range(nq)]

    def tile(arr, c, q):
        return arr[c * t:(c + 1) * t, q * pw:(q + 1) * pw]

    ar, sc = {}, {}
    for cq in tiles:
        ar[cq] = jnp.concatenate([tile(a_t, *cq), tile(r_t, *cq)], axis=0)
        bk = jnp.concatenate([bd(tile(b_h, *cq)), bd(tile(k_h, *cq))], axis=0)
        sc[cq] = _dot_nt(ar[cq], bk)
    l_ak, m_all, x, p = {}, {}, {}, {}
    for cq in tiles:
        l_ab = jnp.where(strict, sc[cq][0:t, 0:pw], 0.0)
        l_ak[cq] = jnp.where(strict, sc[cq][0:t, pw:], 0.0).astype(BF16)
        m_all[cq] = jnp.concatenate(
            [jnp.where(incl, sc[cq][t:, 0:pw], 0.0),
             jnp.where(incl, sc[cq][t:, pw:], 0.0)], axis=1).astype(BF16)
        x[cq] = eye + l_ab
        p[cq] = _dot(l_ab.astype(BF16), bd(l_ab))
    for _ in range(int(np.log2(t)) - 2):
        for cq in tiles:
            z = _dot(jnp.concatenate([p[cq], x[cq]], axis=0).astype(BF16), bd(p[cq]))
            p[cq] = z[0:t]
            x[cq] = x[cq] + z[t:]
    for cq in tiles:
        x[cq] = (x[cq] + _dot(x[cq].astype(BF16), bd(p[cq]))).astype(BF16)

    g_cur = [state_ref[q] for q in range(nq)]
    for c in range(tb // t):
        ag, u = {}, {}
        for q in range(nq):
            ag[q] = _dot_nt(ar[c, q], g_cur[q].astype(BF16))
        for q in range(nq):
            rhs = ag[q][0:t] + _dot(l_ak[c, q], bd(tile(v, c, q)))
            u[q] = _dot(x[c, q], bd(rhs))
        for q in range(nq):
            vq = tile(v, c, q)
            uv = jnp.concatenate([bd(u[q]), bd(vq)], axis=0)
            y_ref[c * t:(c + 1) * t, q * pw:(q + 1) * pw] = (
                ag[q][t:] + _dot(m_all[c, q], uv))
            upd = _dot_tn(jnp.concatenate([u[q], vq], axis=0).astype(BF16),
                          jnp.concatenate([tile(b_w, c, q), tile(k_w, c, q)], axis=0))
            g_cur[q] = jnp.where(
                same_head,
                g_cur[q] * w_end[c * t:c * t + 1, q * pw:(q + 1) * pw] + upd, 0.0)
    for q in range(nq):
        state_ref[q] = g_cur[q]

    y = y_ref[...]
    mu = gsum(y) * (1.0 / n)
    d = y - mu
    var = gsum(d * d) * (1.0 / n)
    yn = d * lax.rsqrt(var + RWKV_GN_EPS) * lnw_ref[...] + lnb_ref[...]
    o_ref[0] = (yn + bonus) * g


def _rwkv(proj_rw, mu, w0, a0, wwa, g2, k_k, k_a, r_k, ln_w, ln_b):
    b, s, _ = proj_rw.shape
    t, n, w, hh, tb = RWKV_CHUNK, RWKV_HEAD_DIM, RWKV_WIDTH, RWKV_HEADS, RWKV_BLOCK
    grp = jnp.arange(w) // n
    gmat = (grp[:, None] == grp[None, :]).astype(BF16)
    ti = jnp.arange(tb)
    tri = ((ti[:, None] >= ti[None, :])
           & (ti[:, None] // t == ti[None, :] // t)).astype(BF16)
    vec = _const_spec((1, w))
    return pl.pallas_call(
        _rwkv_kernel,
        grid=(b, s // tb),
        in_specs=[
            pl.BlockSpec((1, tb, RWKV_COLS), lambda i, j: (i, j, 0)),
            _const_spec((1, RWKV_COLS)),
            vec, vec,
            _const_spec((DECAY_LORA + AAA_LORA, 2 * w)),
            _const_spec((GATE_LORA, w)),
            vec, vec, vec, vec, vec,
            _const_spec((w, w)),
            _const_spec((tb, tb)),
        ],
        out_specs=pl.BlockSpec((1, tb, w), lambda i, j: (i, j, 0)),
        out_shape=jax.ShapeDtypeStruct((b, s, w), F32),
        scratch_shapes=[
            pltpu.VMEM((hh // 2, 2 * n, 2 * n), F32),
            pltpu.VMEM((1, RWKV_COLS), F32),
            pltpu.VMEM((tb, w), F32),
        ],
        compiler_params=pltpu.CompilerParams(
            dimension_semantics=("arbitrary", "arbitrary"),
            vmem_limit_bytes=V7X_VMEM_LIMIT),
        name="rwkv7",
    )(proj_rw, mu, w0, a0, wwa, g2, k_k, k_a, r_k, ln_w, ln_b, gmat, tri)


def _tail_kernel(x_ref, yret_ref, yrw_ref, p_ref, wo_ref, nfw_ref, wg_ref, wu_ref,
                 wd_ref, npw_ref, wpg_ref, wpu_ref, fnw_ref, o_ref):
    x = x_ref[...]
    x = (x + _dot(yret_ref[...].astype(BF16), wo_ref[0:RET_WIDTH, :])
         + _dot(yrw_ref[...].astype(BF16), wo_ref[RET_WIDTH:, :]))
    hf = _rms(x, nfw_ref[...]).astype(BF16)
    gate = _dot(hf, wg_ref[...])
    up = _dot(hf, wu_ref[...])
    act = (gate * _sigmoid(gate) * up).astype(BF16)
    x = x + _dot(act, wd_ref[...])
    hp = _rms(x, npw_ref[...]).astype(BF16)
    pg = _sigmoid(_dot(hp, wpg_ref[...]))
    x = x + _dot(p_ref[...].astype(BF16), wpu_ref[...]) * pg
    o_ref[...] = _rms(x, fnw_ref[...])


def _tail(x2, y_ret, y_rw, p2, w_o, nfw, w_gate, w_up, w_down, npw, w_pg, w_pu, fnw):
    m = x2.shape[0]
    tm = TOKEN_TILE
    vec = _const_spec((1, D_MODEL))
    return pl.pallas_call(
        _tail_kernel,
        grid=(m // tm,),
        in_specs=[
            pl.BlockSpec((tm, D_MODEL), lambda i: (i, 0)),
            pl.BlockSpec((tm, RET_WIDTH), lambda i: (i, 0)),
            pl.BlockSpec((tm, RWKV_WIDTH), lambda i: (i, 0)),
            pl.BlockSpec((tm, D_PLE), lambda i: (i, 0)),
            _const_spec((D_MODEL, D_MODEL)),
            vec,
            _const_spec((D_MODEL, D_FF)),
            _const_spec((D_MODEL, D_FF)),
            _const_spec((D_FF, D_MODEL)),
            vec,
            _const_spec((D_MODEL, D_MODEL)),
            _const_spec((D_PLE, D_MODEL)),
            vec,
        ],
        out_specs=pl.BlockSpec((tm, D_MODEL), lambda i: (i, 0)),
        out_shape=jax.ShapeDtypeStruct((m, D_MODEL), F32),
        compiler_params=pltpu.CompilerParams(
            dimension_semantics=("arbitrary",), vmem_limit_bytes=V7X_VMEM_LIMIT),
        name="tail",
    )(x2, y_ret, y_rw, p2, w_o, nfw, w_gate, w_up, w_down, npw, w_pg, w_pu, fnw)


def kernel(x, p, norm_mix_w, w_in, ret_norm_w, rw_mu, rw_w0, rw_w2, rw_a0, rw_a2,
           rw_g2, rw_k_k, rw_k_a, rw_r_k, rw_ln_w, rw_ln_b, w_o, norm_ffn_w,
           w_gate, w_up, w_down, norm_ple_w, w_ple_gate, w_ple_up, final_norm_w):
    b, s, dm = x.shape
    assert w_in.shape[0] == 1, "single-layer trunk only"
    x2 = x.reshape(b * s, dm)
    row = lambda a: a.reshape(1, -1)
    for i in range(1):
        w_in_b = w_in[i].astype(BF16)
        proj_ret, proj_rw = _in_proj(x2, row(norm_mix_w[i]),
                                     w_in_b[:, :RET_COLS], w_in_b[:, RET_COLS:])
        y_ret = _retention(proj_ret.reshape(b, s, RET_COLS), row(ret_norm_w[i]))
        zeros = jnp.zeros((DECAY_LORA, RWKV_WIDTH), F32)
        wwa = jnp.concatenate([
            jnp.concatenate([rw_w2[i], zeros], axis=1),
            jnp.concatenate([zeros, rw_a2[i]], axis=1)], axis=0).astype(BF16)
        y_rw = _rwkv(proj_rw.reshape(b, s, RWKV_COLS), row(rw_mu[i]), row(rw_w0[i]),
                     row(rw_a0[i]), wwa, rw_g2[i].astype(BF16), row(rw_k_k[i]),
                     row(rw_k_a[i]), row(rw_r_k[i]), row(rw_ln_w[i]), row(rw_ln_b[i]))
        fnw = final_norm_w
        x2 = _tail(x2, y_ret.reshape(b * s, RET_WIDTH), y_rw.reshape(b * s, RWKV_WIDTH),
                   p[i].reshape(b * s, D_PLE), w_o[i].astype(BF16), row(norm_ffn_w[i]),
                   w_gate[i].astype(BF16), w_up[i].astype(BF16), w_down[i].astype(BF16),
                   row(norm_ple_w[i]), w_ple_gate[i].astype(BF16),
                   w_ple_up[i].astype(BF16), row(fnw))
    return x2.reshape(b, s, dm)
```

```python
import functools

import jax
import jax.numpy as jnp
import numpy as np
from jax import lax
from jax.experimental import pallas as pl
from jax.experimental.pallas import tpu as pltpu

F32 = jnp.float32
BF16 = jnp.bfloat16

D_MODEL = 1024
D_PLE = 256
RET_HEAD_DIM = 128
RET_WIDTH = 512
RET_HEADS = 4
RET_CHUNK = 128
RET_COLS = 4 * RET_WIDTH
RWKV_HEAD_DIM = 64
RWKV_WIDTH = 512
RWKV_HEADS = 8
DECAY_LORA = 64
AAA_LORA = 64
GATE_LORA = 128
RWKV_COLS = 3 * RWKV_WIDTH + DECAY_LORA + AAA_LORA + GATE_LORA
D_FF = 2816
NORM_EPS = 1e-6
RET_GN_EPS = 1e-5
RWKV_GN_EPS = 64e-5

RET_BLOCK = 256
RWKV_CHUNK = 64
RWKV_BLOCK = 256
TOKEN_TILE = 256
V7X_VMEM_LIMIT = 56 * 1024 * 1024


def _dot(a, b):
    return jnp.dot(a, b, preferred_element_type=F32)


def _dot_nt(a, b):
    return lax.dot_general(a, b, (((1,), (1,)), ((), ())), preferred_element_type=F32)


def _dot_tn(a, b):
    return lax.dot_general(a, b, (((0,), (0,)), ((), ())), preferred_element_type=F32)


def _rms(x, w):
    ms = jnp.mean(x * x, axis=-1, keepdims=True)
    return x * lax.rsqrt(ms + NORM_EPS) * w


def _sigmoid(x):
    return 1.0 / (1.0 + jnp.exp(-x))


def _const_spec(shape):
    nd = len(shape)
    return pl.BlockSpec(shape, lambda *_: (0,) * nd, pipeline_mode=pl.Buffered(1))


def _in_proj_kernel(x_ref, nw_ref, wret_ref, wrw_ref, oret_ref, orw_ref):
    hn = _rms(x_ref[...], nw_ref[...]).astype(BF16)
    oret_ref[...] = _dot(hn, wret_ref[...])
    orw_ref[...] = _dot(hn, wrw_ref[...])


def _in_proj(x2, nw, w_ret, w_rw):
    m = x2.shape[0]
    tm = TOKEN_TILE
    return pl.pallas_call(
        _in_proj_kernel,
        grid=(m // tm,),
        in_specs=[
            pl.BlockSpec((tm, D_MODEL), lambda i: (i, 0)),
            _const_spec((1, D_MODEL)),
            _const_spec((D_MODEL, RET_COLS)),
            _const_spec((D_MODEL, RWKV_COLS)),
        ],
        out_specs=[
            pl.BlockSpec((tm, RET_COLS), lambda i: (i, 0)),
            pl.BlockSpec((tm, RWKV_COLS), lambda i: (i, 0)),
        ],
        out_shape=[
            jax.ShapeDtypeStruct((m, RET_COLS), F32),
            jax.ShapeDtypeStruct((m, RWKV_COLS), F32),
        ],
        compiler_params=pltpu.CompilerParams(
            dimension_semantics=("arbitrary",), vmem_limit_bytes=V7X_VMEM_LIMIT),
        name="in_proj",
    )(x2, nw, w_ret, w_rw)


def _retention_kernel(p_ref, cos_ref, sin_ref, mask_ref, kte_ref, qfs_ref, cd_ref,
                      nw_ref, ones_ref, o_ref, state_ref):
    c, d = RET_CHUNK, RET_HEAD_DIM
    nchunk = RET_BLOCK // c

    @pl.when(pl.program_id(1) == 0)
    def _():
        state_ref[...] = jnp.zeros_like(state_ref)

    cos = cos_ref[...]
    sin = sin_ref[...]

    def rot(t):
        return t * cos + pltpu.roll(t, d // 2, 1) * sin

    tiles = [(ci, h) for ci in range(nchunk) for h in range(RET_HEADS)]
    q, k, vb = {}, {}, {}
    for h in range(RET_HEADS):
        lo = h * d
        qh = rot(p_ref[0, :, lo:lo + d])
        kh = rot(p_ref[0, :, RET_WIDTH + lo:RET_WIDTH + lo + d])
        vh = p_ref[0, :, 2 * RET_WIDTH + lo:2 * RET_WIDTH + lo + d].astype(BF16)
        for ci in range(nchunk):
            rows = slice(ci * c, (ci + 1) * c)
            q[ci, h], k[ci, h], vb[ci, h] = qh[rows], kh[rows], vh[rows]
    lhs, kd = {}, {}
    for cq in tiles:
        h = cq[1]
        scores = _dot_nt(q[cq].astype(BF16), k[cq].astype(BF16)) * mask_ref[h]
        lhs[cq] = jnp.concatenate([scores, q[cq] * qfs_ref[h]], axis=1).astype(BF16)
        kd[cq] = (k[cq] * kte_ref[h]).astype(BF16)
    r_cur = [state_ref[h] for h in range(RET_HEADS)]
    ys = []
    for ci in range(nchunk):
        for h in range(RET_HEADS):
            rhs = jnp.concatenate([vb[ci, h], r_cur[h].astype(BF16)], axis=0)
            ys.append(_dot(lhs[ci, h], rhs))
            r_cur[h] = r_cur[h] * cd_ref[h] + _dot_tn(kd[ci, h], vb[ci, h])
    for h in range(RET_HEADS):
        state_ref[h] = r_cur[h]

    ones = ones_ref[...]
    y = jnp.concatenate(ys, axis=0)
    dlt = y - _dot(y.astype(BF16), ones)
    var = _dot((dlt * dlt).astype(BF16), ones)
    yn = dlt * lax.rsqrt(var + RET_GN_EPS)
    for i, (ci, h) in enumerate(tiles):
        lo = h * d
        g = p_ref[0, ci * c:(ci + 1) * c, 3 * RET_WIDTH + lo:3 * RET_WIDTH + lo + d]
        o_ref[0, ci * c:(ci + 1) * c, lo:lo + d] = (
            yn[i * c:(i + 1) * c] * nw_ref[:, lo:lo + d] * (g * _sigmoid(g)))


def _retention_tables(seq):
    d, c, hh = RET_HEAD_DIM, RET_CHUNK, RET_HEADS
    pos = jnp.arange(seq, dtype=F32)
    angle = 1.0 / (10000.0 ** jnp.linspace(0.0, 1.0, d // 2, dtype=F32))
    phase = pos[:, None] * angle[None, :]
    cos = jnp.concatenate([jnp.cos(phase), jnp.cos(phase)], axis=1)
    sin = jnp.concatenate([-jnp.sin(phase), jnp.sin(phase)], axis=1)
    log_g = jnp.log(1.0 - 2.0 ** (-5.0 - jnp.arange(hh, dtype=F32)))
    idx = jnp.arange(c, dtype=F32)
    rel = idx[:, None] - idx[None, :]
    scale = d ** -0.5
    mask = jnp.where(rel[None] >= 0,
                     jnp.exp(log_g[:, None, None] * jnp.maximum(rel, 0.0)[None]), 0.0) * scale
    kte = jnp.exp(log_g[:, None] * (c - 1 - idx)[None]) * scale
    kte = jnp.broadcast_to(kte[:, :, None], (hh, c, d))
    qfs = jnp.exp(log_g[:, None] * (idx + 1.0)[None])
    qfs = jnp.broadcast_to(qfs[:, :, None], (hh, c, d))
    cd = jnp.broadcast_to(jnp.exp(log_g * c)[:, None, None], (hh, 1, d))
    return cos, sin, mask, kte, qfs, cd


def _even_odd_columns(w_cols):
    rows = w_cols.shape[0]
    w4 = w_cols.reshape(rows, -1, RET_HEAD_DIM // 2, 2)
    return jnp.swapaxes(w4, 2, 3).reshape(rows, -1)


def _retention(proj_ret, ret_norm_w):
    b, s, _ = proj_ret.shape
    c, d, hh, tb = RET_CHUNK, RET_HEAD_DIM, RET_HEADS, RET_BLOCK
    cos, sin, mask, kte, qfs, cd = _retention_tables(s)
    ones = jnp.full((d, d), 1.0 / d, BF16)
    tab_spec = pl.BlockSpec((tb, d), lambda i, j: (j, 0))
    return pl.pallas_call(
        _retention_kernel,
        grid=(b, s // tb),
        in_specs=[
            pl.BlockSpec((1, tb, RET_COLS), lambda i, j: (i, j, 0)),
            tab_spec, tab_spec,
            _const_spec((hh, c, c)),
            _const_spec((hh, c, d)),
            _const_spec((hh, c, d)),
            _const_spec((hh, 1, d)),
            _const_spec((1, RET_WIDTH)),
            _const_spec((d, d)),
        ],
        out_specs=pl.BlockSpec((1, tb, RET_WIDTH), lambda i, j: (i, j, 0)),
        out_shape=jax.ShapeDtypeStruct((b, s, RET_WIDTH), F32),
        scratch_shapes=[pltpu.VMEM((hh, d, d), F32)],
        compiler_params=pltpu.CompilerParams(
            dimension_semantics=("arbitrary", "arbitrary"),
            vmem_limit_bytes=V7X_VMEM_LIMIT),
        name="retention",
    )(proj_ret, cos, sin, mask, kte, qfs, cd, ret_norm_w, ones)


def _split3(x):
    hi = x.astype(BF16)
    r1 = x - hi.astype(F32)
    mid = r1.astype(BF16)
    lo = (r1 - mid.astype(F32)).astype(BF16)
    return hi, mid, lo


def _rwkv_kernel(p_ref, mu_ref, w0_ref, a0_ref, wwa_ref, g2_ref, kk_ref, ka_ref,
                 rk_ref, lnw_ref, lnb_ref, gsum_ref, tri_ref, o_ref, state_ref,
                 carry_ref, y_ref):
    t = RWKV_CHUNK
    n = RWKV_HEAD_DIM
    w = RWKV_WIDTH
    tb = RWKV_BLOCK
    pw = 2 * n

    @pl.when(pl.program_id(1) == 0)
    def _():
        state_ref[...] = jnp.zeros_like(state_ref)
        carry_ref[...] = jnp.zeros_like(carry_ref)

    h = p_ref[0]
    row = lax.broadcasted_iota(jnp.int32, h.shape, 0)
    prev = jnp.where(row == 0, carry_ref[...], pltpu.roll(h, 1, 0))
    carry_ref[...] = h[tb - 1:tb, :]
    hs = h + (prev - h) * mu_ref[...]

    r = hs[:, 0:w]
    k = hs[:, w:2 * w]
    v = hs[:, 2 * w:3 * w]
    wa = hs[:, 3 * w:3 * w + DECAY_LORA + AAA_LORA]
    gd = hs[:, 3 * w + DECAY_LORA + AAA_LORA:]
    lane = lax.broadcasted_iota(jnp.int32, wa.shape, 1)
    wa = jnp.where(lane < DECAY_LORA, jnp.tanh(wa), wa)
    lora = _dot(wa.astype(BF16), wwa_ref[...])
    wp = w0_ref[...] + lora[:, :w]
    logw = (-float(np.exp(-0.5))) * _sigmoid(wp)
    a = _sigmoid(a0_ref[...] + lora[:, w:])
    g = _dot(_sigmoid(gd).astype(BF16), g2_ref[...])

    gmat = gsum_ref[...]
    half = w // 2

    def gsum(xs):
        stack = jnp.concatenate(
            [x[:, i * half:(i + 1) * half] for x in xs for i in range(2)], axis=0)
        res = _dot(stack.astype(BF16), gmat)
        return [jnp.concatenate([res[2 * j * tb:(2 * j + 1) * tb],
                                 res[(2 * j + 1) * tb:(2 * j + 2) * tb]], axis=1)
                for j in range(len(xs))]

    kk = k * kk_ref[...]
    k2 = k * (1.0 + (a - 1.0) * ka_ref[...])
    kk_sq, rk_sum = gsum([kk * kk, r * k2 * rk_ref[...]])
    kk = kk * lax.rsqrt(jnp.maximum(kk_sq, 1e-24))
    bonus = rk_sum * v
    bb = kk * a

    tri = tri_ref[...]
    l_hi, l_mid, l_lo = _split3(logw)
    cum = _dot(tri, l_hi) + _dot(tri, l_mid) + _dot(tri, l_lo)
    cum_end = jnp.concatenate(
        [jnp.broadcast_to(cum[(c + 1) * t - 1:(c + 1) * t, :], (t, w))
         for c in range(tb // t)], axis=0)
    e_neg = jnp.exp(-cum)
    a_f = -kk * jnp.exp(cum - logw)
    r_f = r * jnp.exp(cum)
    a_t = a_f.astype(BF16)
    r_t = r_f.astype(BF16)
    b_h = bb * e_neg
    k_h = k2 * e_neg
    to_end = jnp.exp(cum_end - cum)
    b_w = bb * to_end
    k_w = k2 * to_end
    w_end = jnp.exp(cum_end)

    ri = lax.broadcasted_iota(jnp.int32, (t, pw), 0)
    ci = lax.broadcasted_iota(jnp.int32, (t, pw), 1)
    cm = jnp.bitwise_and(ci, t - 1)
    incl = ri >= cm
    strict = ri > cm
    eye = jnp.where(ri == cm, 1.0, 0.0)
    lane0 = ci < n
    si = lax.broadcasted_iota(jnp.int32, (pw, pw), 0)
    sj = lax.broadcasted_iota(jnp.int32, (pw, pw), 1)
    same_head = (si < n) == (sj < n)

    def bd(x):
        return jnp.concatenate(
            [jnp.where(lane0, x, 0.0), jnp.where(lane0, 0.0, x)], axis=0).astype(BF16)

    nq = RWKV_HEADS // 2
    tiles = [(c, q) for c in range(tb // t) for q in range(nq)]

    def tile(arr, c, q):
        return arr[c * t:(c + 1) * t, q * pw:(q + 1) * pw]

    ar, sc = {}, {}
    for cq in tiles:
        ar[cq] = jnp.concatenate([tile(a_t, *cq), tile(r_t, *cq)], axis=0)
        bk = jnp.concatenate([bd(tile(b_h, *cq)), bd(tile(k_h, *cq))], axis=0)
        sc[cq] = _dot_nt(ar[cq], bk)
    lm, m_rb, x, p = {}, {}, {}, {}
    for cq in tiles:
        l_ab = jnp.where(strict, sc[cq][0:t, 0:pw], 0.0)
        lm[cq] = jnp.concatenate(
            [jnp.where(strict, sc[cq][0:t, pw:], 0.0),
             jnp.where(incl, sc[cq][t:, pw:], 0.0)], axis=0).astype(BF16)
        m_rb[cq] = jnp.where(incl, sc[cq][t:, 0:pw], 0.0).astype(BF16)
        x[cq] = eye + l_ab
        p[cq] = _dot(l_ab.astype(BF16), bd(l_ab))
    for _ in range(int(np.log2(t)) - 2):
        for cq in tiles:
            z = _dot(jnp.concatenate([p[cq], x[cq]], axis=0).astype(BF16), bd(p[cq]))
            p[cq] = z[0:t]
            x[cq] = x[cq] + z[t:]
    for cq in tiles:
        x[cq] = (x[cq] + _dot(x[cq].astype(BF16), bd(p[cq]))).astype(BF16)

    lvov, xac = {}, {}
    for cq in tiles:
        lvov[cq] = _dot(lm[cq], bd(tile(v, *cq)))
    for cq in tiles:
        rhs = jnp.concatenate([bd(tile(a_f, *cq)), bd(lvov[cq][0:t])], axis=1)
        xac[cq] = _dot(x[cq], rhs)
    mq, nn, oc, wcb = {}, {}, {}, {}
    zero_tile = jnp.zeros((t, pw), F32)
    for cq in tiles:
        bkt = jnp.transpose(
            jnp.concatenate([tile(b_w, *cq), tile(k_w, *cq)], axis=0)).astype(BF16)
        low = jnp.concatenate([zero_tile, tile(v, *cq)], axis=1)
        mn = _dot(bkt, jnp.concatenate([xac[cq], low], axis=0).astype(BF16))
        qo = _dot(m_rb[cq], jnp.concatenate(
            [bd(xac[cq][:, 0:pw]), bd(xac[cq][:, pw:])], axis=1))
        mq[cq] = jnp.concatenate(
            [jnp.where(same_head, mn[:, 0:pw], 0.0),
             tile(r_f, *cq) + qo[:, 0:pw]], axis=0).astype(BF16)
        nn[cq] = jnp.where(same_head, mn[:, pw:], 0.0)
        oc[cq] = qo[:, pw:] + lvov[cq][t:]
        c, q = cq
        wcb[cq] = jnp.transpose(jnp.broadcast_to(
            w_end[c * t:c * t + 1, q * pw:(q + 1) * pw], (pw, pw)))

    h_cur = [state_ref[q] for q in range(nq)]
    for c in range(tb // t):
        for q in range(nq):
            mh = _dot(mq[c, q], h_cur[q].astype(BF16))
            y_ref[c * t:(c + 1) * t, q * pw:(q + 1) * pw] = mh[pw:] + oc[c, q]
            h_cur[q] = wcb[c, q] * h_cur[q] + mh[0:pw] + nn[c, q]
    for q in range(nq):
        state_ref[q] = h_cur[q]

    y = y_ref[...]
    d = y - gsum([y])[0] * (1.0 / n)
    var = gsum([d * d])[0] * (1.0 / n)
    yn = d * lax.rsqrt(var + RWKV_GN_EPS) * lnw_ref[...] + lnb_ref[...]
    o_ref[0] = (yn + bonus) * g


def _rwkv(proj_rw, mu, w0, a0, wwa, g2, k_k, k_a, r_k, ln_w, ln_b):
    b, s, _ = proj_rw.shape
    t, n, w, hh, tb = RWKV_CHUNK, RWKV_HEAD_DIM, RWKV_WIDTH, RWKV_HEADS, RWKV_BLOCK
    grp = jnp.arange(w // 2) // n
    gmat = (grp[:, None] == grp[None, :]).astype(BF16)
    ti = jnp.arange(tb)
    tri = ((ti[:, None] >= ti[None, :])
           & (ti[:, None] // t == ti[None, :] // t)).astype(BF16)
    vec = _const_spec((1, w))
    return pl.pallas_call(
        _rwkv_kernel,
        grid=(b, s // tb),
        in_specs=[
            pl.BlockSpec((1, tb, RWKV_COLS), lambda i, j: (i, j, 0)),
            _const_spec((1, RWKV_COLS)),
            vec, vec,
            _const_spec((DECAY_LORA + AAA_LORA, 2 * w)),
            _const_spec((GATE_LORA, w)),
            vec, vec, vec, vec, vec,
            _const_spec((w // 2, w // 2)),
            _const_spec((tb, tb)),
        ],
        out_specs=pl.BlockSpec((1, tb, w), lambda i, j: (i, j, 0)),
        out_shape=jax.ShapeDtypeStruct((b, s, w), F32),
        scratch_shapes=[
            pltpu.VMEM((hh // 2, 2 * n, 2 * n), F32),
            pltpu.VMEM((1, RWKV_COLS), F32),
            pltpu.VMEM((tb, w), F32),
        ],
        compiler_params=pltpu.CompilerParams(
            dimension_semantics=("arbitrary", "arbitrary"),
            vmem_limit_bytes=V7X_VMEM_LIMIT),
        name="rwkv7",
    )(proj_rw, mu, w0, a0, wwa, g2, k_k, k_a, r_k, ln_w, ln_b, gmat, tri)


def _tail_kernel(x_ref, yret_ref, yrw_ref, p_ref, wo_ref, nfw_ref, wg_ref, wu_ref,
                 wd_ref, npw_ref, wpg_ref, wpu_ref, fnw_ref, o_ref):
    x = x_ref[...]
    x = (x + _dot(yret_ref[...].astype(BF16), wo_ref[0:RET_WIDTH, :])
         + _dot(yrw_ref[...].astype(BF16), wo_ref[RET_WIDTH:, :]))
    hf = _rms(x, nfw_ref[...]).astype(BF16)
    gate = _dot(hf, wg_ref[...])
    up = _dot(hf, wu_ref[...])
    act = (gate * _sigmoid(gate) * up).astype(BF16)
    x = x + _dot(act, wd_ref[...])
    hp = _rms(x, npw_ref[...]).astype(BF16)
    pg = _sigmoid(_dot(hp, wpg_ref[...]))
    x = x + _dot(p_ref[...].astype(BF16), wpu_ref[...]) * pg
    o_ref[...] = _rms(x, fnw_ref[...])


def _tail(x2, y_ret, y_rw, p2, w_o, nfw, w_gate, w_up, w_down, npw, w_pg, w_pu, fnw):
    m = x2.shape[0]
    tm = TOKEN_TILE
    vec = _const_spec((1, D_MODEL))
    return pl.pallas_call(
        _tail_kernel,
        grid=(m // tm,),
        in_specs=[
            pl.BlockSpec((tm, D_MODEL), lambda i: (i, 0)),
            pl.BlockSpec((tm, RET_WIDTH), lambda i: (i, 0)),
            pl.BlockSpec((tm, RWKV_WIDTH), lambda i: (i, 0)),
            pl.BlockSpec((tm, D_PLE), lambda i: (i, 0)),
            _const_spec((D_MODEL, D_MODEL)),
            vec,
            _const_spec((D_MODEL, D_FF)),
            _const_spec((D_MODEL, D_FF)),
            _const_spec((D_FF, D_MODEL)),
            vec,
            _const_spec((D_MODEL, D_MODEL)),
            _const_spec((D_PLE, D_MODEL)),
            vec,
        ],
        out_specs=pl.BlockSpec((tm, D_MODEL), lambda i: (i, 0)),
        out_shape=jax.ShapeDtypeStruct((m, D_MODEL), F32),
        compiler_params=pltpu.CompilerParams(
            dimension_semantics=("arbitrary",), vmem_limit_bytes=V7X_VMEM_LIMIT),
        name="tail",
    )(x2, y_ret, y_rw, p2, w_o, nfw, w_gate, w_up, w_down, npw, w_pg, w_pu, fnw)


def kernel(x, p, norm_mix_w, w_in, ret_norm_w, rw_mu, rw_w0, rw_w2, rw_a0, rw_a2,
           rw_g2, rw_k_k, rw_k_a, rw_r_k, rw_ln_w, rw_ln_b, w_o, norm_ffn_w,
           w_gate, w_up, w_down, norm_ple_w, w_ple_gate, w_ple_up, final_norm_w):
    b, s, dm = x.shape
    assert w_in.shape[0] == 1, "single-layer trunk only"
    x2 = x.reshape(b * s, dm)
    row = lambda a: a.reshape(1, -1)
    for i in range(1):
        w_in_b = w_in[i].astype(BF16)
        w_ret = jnp.concatenate([_even_odd_columns(w_in_b[:, :2 * RET_WIDTH]),
                                 w_in_b[:, 2 * RET_WIDTH:RET_COLS]], axis=1)
        proj_ret, proj_rw = _in_proj(x2, row(norm_mix_w[i]), w_ret, w_in_b[:, RET_COLS:])
        y_ret = _retention(proj_ret.reshape(b, s, RET_COLS), row(ret_norm_w[i]))
        zeros = jnp.zeros((DECAY_LORA, RWKV_WIDTH), F32)
        wwa = jnp.concatenate([
            jnp.concatenate([rw_w2[i], zeros], axis=1),
            jnp.concatenate([zeros, rw_a2[i]], axis=1)], axis=0).astype(BF16)
        y_rw = _rwkv(proj_rw.reshape(b, s, RWKV_COLS), row(rw_mu[i]), row(rw_w0[i]),
                     row(rw_a0[i]), wwa, rw_g2[i].astype(BF16), row(rw_k_k[i]),
                     row(rw_k_a[i]), row(rw_r_k[i]), row(rw_ln_w[i]), row(rw_ln_b[i]))
        fnw = final_norm_w
        x2 = _tail(x2, y_ret.reshape(b * s, RET_WIDTH), y_rw.reshape(b * s, RWKV_WIDTH),
                   p[i].reshape(b * s, D_PLE), w_o[i].astype(BF16), row(norm_ffn_w[i]),
                   w_gate[i].astype(BF16), w_up[i].astype(BF16), w_down[i].astype(BF16),
                   row(norm_ple_w[i]), w_ple_gate[i].astype(BF16),
                   w_ple_up[i].astype(BF16), row(fnw))
    return x2.reshape(b, s, dm)
```

```python
import functools

import jax
import jax.numpy as jnp
import numpy as np
from jax import lax
from jax.experimental import pallas as pl
from jax.experimental.pallas import tpu as pltpu

F32 = jnp.float32
BF16 = jnp.bfloat16

D_MODEL = 1024
D_PLE = 256
RET_HEAD_DIM = 128
RET_WIDTH = 512
RET_HEADS = 4
RET_CHUNK = 128
RET_COLS = 4 * RET_WIDTH
RWKV_HEAD_DIM = 64
RWKV_WIDTH = 512
RWKV_HEADS = 8
DECAY_LORA = 64
AAA_LORA = 64
GATE_LORA = 128
RWKV_COLS = 3 * RWKV_WIDTH + DECAY_LORA + AAA_LORA + GATE_LORA
IN_COLS = RET_COLS + RWKV_COLS
D_FF = 2816
NORM_EPS = 1e-6
RET_GN_EPS = 1e-5
RWKV_GN_EPS = 64e-5

RWKV_CHUNK = 64
MIX_BLOCK = 256
PROJ_CHUNK = 256
TOKEN_TILE = 256
V7X_VMEM_LIMIT = 56 * 1024 * 1024


def _dot(a, b):
    return jnp.dot(a, b, preferred_element_type=F32)


def _dot_nt(a, b):
    return lax.dot_general(a, b, (((1,), (1,)), ((), ())), preferred_element_type=F32)


def _dot_tn(a, b):
    return lax.dot_general(a, b, (((0,), (0,)), ((), ())), preferred_element_type=F32)


def _rms(x, w):
    ms = jnp.mean(x * x, axis=-1, keepdims=True)
    return x * lax.rsqrt(ms + NORM_EPS) * w


def _sigmoid(x):
    return 1.0 / (1.0 + jnp.exp(-x))


def _const_spec(shape):
    nd = len(shape)
    return pl.BlockSpec(shape, lambda *_: (0,) * nd, pipeline_mode=pl.Buffered(1))


def _retention_block(load, cos, sin, mask_ref, kte_ref, qfs_ref, cd_ref, nw_ref, ones,
                     state_ref, fill):
    c, d = RET_CHUNK, RET_HEAD_DIM
    nchunk = MIX_BLOCK // c

    def rot(t):
        return t * cos + pltpu.roll(t, d // 2, 1) * sin

    tiles = [(ci, h) for ci in range(nchunk) for h in range(RET_HEADS)]
    q, k, vb, gate = {}, {}, {}, {}
    for h in range(RET_HEADS):
        lo = h * d
        qh = rot(load(lo, lo + d))
        kh = rot(load(RET_WIDTH + lo, RET_WIDTH + lo + d))
        vh = load(2 * RET_WIDTH + lo, 2 * RET_WIDTH + lo + d).astype(BF16)
        gh = load(3 * RET_WIDTH + lo, 3 * RET_WIDTH + lo + d)
        gh = gh * _sigmoid(gh) * nw_ref[:, lo:lo + d]
        for ci in range(nchunk):
            rows = slice(ci * c, (ci + 1) * c)
            q[ci, h], k[ci, h], vb[ci, h], gate[ci, h] = qh[rows], kh[rows], vh[rows], gh[rows]
    fill(1)
    lhs, kd = {}, {}
    for cq in tiles:
        h = cq[1]
        scores = _dot_nt(q[cq].astype(BF16), k[cq].astype(BF16)) * mask_ref[h]
        lhs[cq] = jnp.concatenate([scores, q[cq] * qfs_ref[h]], axis=1).astype(BF16)
        kd[cq] = (k[cq] * kte_ref[h]).astype(BF16)
    r_cur = [state_ref[h] for h in range(RET_HEADS)]
    ys = []
    for ci in range(nchunk):
        for h in range(RET_HEADS):
            rhs = jnp.concatenate([vb[ci, h], r_cur[h].astype(BF16)], axis=0)
            ys.append(_dot(lhs[ci, h], rhs))
            r_cur[h] = r_cur[h] * cd_ref[h] + _dot_tn(kd[ci, h], vb[ci, h])
    for h in range(RET_HEADS):
        state_ref[h] = r_cur[h]

    y = jnp.concatenate(ys, axis=0)
    dlt = y - _dot(y.astype(BF16), ones)
    var = _dot((dlt * dlt).astype(BF16), ones)
    yn = dlt * lax.rsqrt(var + RET_GN_EPS)
    return tiles, [yn[i * c:(i + 1) * c] * gate[cq] for i, cq in enumerate(tiles)]


def _retention_tables(seq):
    d, c, hh = RET_HEAD_DIM, RET_CHUNK, RET_HEADS
    pos = jnp.arange(seq, dtype=F32)
    angle = 1.0 / (10000.0 ** jnp.linspace(0.0, 1.0, d // 2, dtype=F32))
    phase = pos[:, None] * angle[None, :]
    cos = jnp.concatenate([jnp.cos(phase), jnp.cos(phase)], axis=1)
    sin = jnp.concatenate([-jnp.sin(phase), jnp.sin(phase)], axis=1)
    log_g = jnp.log(1.0 - 2.0 ** (-5.0 - jnp.arange(hh, dtype=F32)))
    idx = jnp.arange(c, dtype=F32)
    rel = idx[:, None] - idx[None, :]
    scale = d ** -0.5
    mask = jnp.where(rel[None] >= 0,
                     jnp.exp(log_g[:, None, None] * jnp.maximum(rel, 0.0)[None]), 0.0) * scale
    kte = jnp.exp(log_g[:, None] * (c - 1 - idx)[None]) * scale
    kte = jnp.broadcast_to(kte[:, :, None], (hh, c, d))
    qfs = jnp.exp(log_g[:, None] * (idx + 1.0)[None])
    qfs = jnp.broadcast_to(qfs[:, :, None], (hh, c, d))
    cd = jnp.broadcast_to(jnp.exp(log_g * c)[:, None, None], (hh, 1, d))
    return cos, sin, mask, kte, qfs, cd


def _even_odd_columns(w_cols):
    rows = w_cols.shape[0]
    w4 = w_cols.reshape(rows, -1, RET_HEAD_DIM // 2, 2)
    return jnp.swapaxes(w4, 2, 3).reshape(rows, -1)


def _split3(x):
    hi = x.astype(BF16)
    r1 = x - hi.astype(F32)
    mid = r1.astype(BF16)
    lo = (r1 - mid.astype(F32)).astype(BF16)
    return hi, mid, lo


def _rwkv_block(h, mu_ref, w0_ref, a0_ref, wwa_ref, g2_ref, kk_ref, ka_ref, rk_ref,
                lnw_ref, lnb_ref, gsum_ref, tri_ref, state_ref, carry_ref, y_ref, fill):
    t = RWKV_CHUNK
    n = RWKV_HEAD_DIM
    w = RWKV_WIDTH
    tb = MIX_BLOCK
    pw = 2 * n

    row = lax.broadcasted_iota(jnp.int32, h.shape, 0)
    prev = jnp.where(row == 0, carry_ref[...], pltpu.roll(h, 1, 0))
    carry_ref[...] = h[tb - 1:tb, :]
    hs = h + (prev - h) * mu_ref[...]
    fill(2)

    r = hs[:, 0:w]
    k = hs[:, w:2 * w]
    v = hs[:, 2 * w:3 * w]
    wa = hs[:, 3 * w:3 * w + DECAY_LORA + AAA_LORA]
    gd = hs[:, 3 * w + DECAY_LORA + AAA_LORA:]
    lane = lax.broadcasted_iota(jnp.int32, wa.shape, 1)
    wa = jnp.where(lane < DECAY_LORA, jnp.tanh(wa), wa)
    lora = _dot(wa.astype(BF16), wwa_ref[...])
    fill(1)
    wp = w0_ref[...] + lora[:, :w]
    logw = (-float(np.exp(-0.5))) * _sigmoid(wp)
    a = _sigmoid(a0_ref[...] + lora[:, w:])
    g = _dot(_sigmoid(gd).astype(BF16), g2_ref[...])
    fill(2)

    gmat = gsum_ref[...]
    half = w // 2

    def gsum(xs):
        stack = jnp.concatenate(
            [x[:, i * half:(i + 1) * half] for x in xs for i in range(2)], axis=0)
        res = _dot(stack.astype(BF16), gmat)
        return [jnp.concatenate([res[2 * j * tb:(2 * j + 1) * tb],
                                 res[(2 * j + 1) * tb:(2 * j + 2) * tb]], axis=1)
                for j in range(len(xs))]

    kk = k * kk_ref[...]
    k2 = k * (1.0 + (a - 1.0) * ka_ref[...])
    kk_sq, rk_sum = gsum([kk * kk, r * k2 * rk_ref[...]])
    kk = kk * lax.rsqrt(jnp.maximum(kk_sq, 1e-24))
    bonus = rk_sum * v
    bb = kk * a
    fill(2)

    tri = tri_ref[...]
    l_hi, l_mid, l_lo = _split3(logw)
    cum = _dot(tri, l_hi) + _dot(tri, l_mid) + _dot(tri, l_lo)
    fill(1)
    cum_end = jnp.concatenate(
        [jnp.broadcast_to(cum[(c + 1) * t - 1:(c + 1) * t, :], (t, w))
         for c in range(tb // t)], axis=0)
    e_neg = jnp.exp(-cum)
    a_f = -kk * jnp.exp(cum - logw)
    r_f = r * jnp.exp(cum)
    a_t = a_f.astype(BF16)
    r_t = r_f.astype(BF16)
    b_h = bb * e_neg
    k_h = k2 * e_neg
    to_end = jnp.exp(cum_end - cum)
    b_w = bb * to_end
    k_w = k2 * to_end
    w_end = jnp.exp(cum_end)
    fill(2)

    ri = lax.broadcasted_iota(jnp.int32, (t, pw), 0)
    ci = lax.broadcasted_iota(jnp.int32, (t, pw), 1)
    cm = jnp.bitwise_and(ci, t - 1)
    incl = ri >= cm
    strict = ri > cm
    eye = jnp.where(ri == cm, 1.0, 0.0)
    lane0 = ci < n
    si = lax.broadcasted_iota(jnp.int32, (pw, pw), 0)
    sj = lax.broadcasted_iota(jnp.int32, (pw, pw), 1)
    same_head = (si < n) == (sj < n)

    def bd(x):
        return jnp.concatenate(
            [jnp.where(lane0, x, 0.0), jnp.where(lane0, 0.0, x)], axis=0).astype(BF16)

    nq = RWKV_HEADS // 2
    tiles = [(c, q) for c in range(tb // t) for q in range(nq)]

    def tile(arr, c, q):
        return arr[c * t:(c + 1) * t, q * pw:(q + 1) * pw]

    sc = {}
    for cq in tiles:
        ar = jnp.concatenate([tile(a_t, *cq), tile(r_t, *cq)], axis=0)
        bk = jnp.concatenate([bd(tile(b_h, *cq)), bd(tile(k_h, *cq))], axis=0)
        sc[cq] = _dot_nt(ar, bk)
    lm, m_rb, x, p = {}, {}, {}, {}
    for cq in tiles:
        l_ab = jnp.where(strict, sc[cq][0:t, 0:pw], 0.0)
        lm[cq] = jnp.concatenate(
            [jnp.where(strict, sc[cq][0:t, pw:], 0.0),
             jnp.where(incl, sc[cq][t:, pw:], 0.0)], axis=0).astype(BF16)
        m_rb[cq] = jnp.where(incl, sc[cq][t:, 0:pw], 0.0).astype(BF16)
        x[cq] = eye + l_ab
        p[cq] = _dot(l_ab.astype(BF16), bd(l_ab))
    for _ in range(int(np.log2(t)) - 2):
        for cq in tiles:
            z = _dot(jnp.concatenate([p[cq], x[cq]], axis=0).astype(BF16), bd(p[cq]))
            p[cq] = z[0:t]
            x[cq] = x[cq] + z[t:]
    for cq in tiles:
        x[cq] = (x[cq] + _dot(x[cq].astype(BF16), bd(p[cq]))).astype(BF16)

    lvov, xac = {}, {}
    for cq in tiles:
        lvov[cq] = _dot(lm[cq], bd(tile(v, *cq)))
    for cq in tiles:
        rhs = jnp.concatenate([bd(tile(a_f, *cq)), bd(lvov[cq][0:t])], axis=1)
        xac[cq] = _dot(x[cq], rhs)
    mq, nn, oc, wcb = {}, {}, {}, {}
    zero_tile = jnp.zeros((t, pw), F32)
    for cq in tiles:
        bkt = jnp.transpose(
            jnp.concatenate([tile(b_w, *cq), tile(k_w, *cq)], axis=0)).astype(BF16)
        low = jnp.concatenate([zero_tile, tile(v, *cq)], axis=1)
        mn = _dot(bkt, jnp.concatenate([xac[cq], low], axis=0).astype(BF16))
        qo = _dot(m_rb[cq], jnp.concatenate(
            [bd(xac[cq][:, 0:pw]), bd(xac[cq][:, pw:])], axis=1))
        mq[cq] = jnp.concatenate(
            [jnp.where(same_head, mn[:, 0:pw], 0.0),
             tile(r_f, *cq) + qo[:, 0:pw]], axis=0).astype(BF16)
        nn[cq] = jnp.where(same_head, mn[:, pw:], 0.0)
        oc[cq] = qo[:, pw:] + lvov[cq][t:]
        c, q = cq
        wcb[cq] = jnp.transpose(jnp.broadcast_to(
            w_end[c * t:c * t + 1, q * pw:(q + 1) * pw], (pw, pw)))

    fill(1)
    h_cur = [state_ref[q] for q in range(nq)]
    for c in range(tb // t):
        for q in range(nq):
            mh = _dot(mq[c, q], h_cur[q].astype(BF16))
            y_ref[c * t:(c + 1) * t, q * pw:(q + 1) * pw] = mh[pw:] + oc[c, q]
            h_cur[q] = wcb[c, q] * h_cur[q] + mh[0:pw] + nn[c, q]
    for q in range(nq):
        state_ref[q] = h_cur[q]

    fill(2)
    y = y_ref[...]
    d = y - gsum([y])[0] * (1.0 / n)
    var = gsum([d * d])[0] * (1.0 / n)
    yn = d * lax.rsqrt(var + RWKV_GN_EPS) * lnw_ref[...] + lnb_ref[...]
    return (yn + bonus) * g


def _mixer_kernel(blocks_per_row,
                  x_ref, nw_ref, win_ref, cos_ref, sin_ref, mask_ref, kte_ref, qfs_ref,
                  cd_ref, rnw_ref, ones_ref, mu_ref, w0_ref, a0_ref, wwa_ref, g2_ref,
                  kk_ref, ka_ref, rk_ref, lnw_ref, lnb_ref, gsum_ref, tri_ref,
                  o_ref, proj_a_ref, proj_b_ref, ret_state_ref, rw_state_ref, carry_ref,
                  y_ref):
    i = pl.program_id(0)
    d = RET_HEAD_DIM

    @pl.when(i == 0)
    def _():
        proj_b_ref[...] = jnp.zeros_like(proj_b_ref)

    @pl.when(jnp.logical_or(i == 0, (i + blocks_per_row - 1) % blocks_per_row == 0))
    def _():
        ret_state_ref[...] = jnp.zeros_like(ret_state_ref)
        rw_state_ref[...] = jnp.zeros_like(rw_state_ref)
        carry_ref[...] = jnp.zeros_like(carry_ref)

    def step(new_ref, mix_ref):
        hn = _rms(x_ref[...], nw_ref[...]).astype(BF16)
        cols = iter(range(0, IN_COLS, PROJ_CHUNK))

        def fill(count):
            for _ in range(count):
                lo = next(cols, None)
                if lo is not None:
                    new_ref[:, lo:lo + PROJ_CHUNK] = _dot(hn, win_ref[:, lo:lo + PROJ_CHUNK])

        y_rw = _rwkv_block(mix_ref[:, RET_COLS:], mu_ref, w0_ref, a0_ref,
                           wwa_ref, g2_ref, kk_ref, ka_ref, rk_ref, lnw_ref, lnb_ref,
                           gsum_ref, tri_ref, rw_state_ref, carry_ref, y_ref, fill)
        o_ref[:, RET_WIDTH:] = y_rw.astype(BF16)

        tiles, y_ret = _retention_block(
            lambda lo, hi: mix_ref[:, lo:hi], cos_ref[...], sin_ref[...],
            mask_ref, kte_ref, qfs_ref, cd_ref, rnw_ref, ones_ref[...], ret_state_ref, fill)
        for (ci, h), yt in zip(tiles, y_ret):
            o_ref[ci * RET_CHUNK:(ci + 1) * RET_CHUNK, h * d:(h + 1) * d] = yt.astype(BF16)
        fill(IN_COLS // PROJ_CHUNK)

    pl.when(i % 2 == 0)(functools.partial(step, proj_a_ref, proj_b_ref))
    pl.when(i % 2 == 1)(functools.partial(step, proj_b_ref, proj_a_ref))


def _mixer(x2, seq, nw, w_in_b, rnw, mu, w0, a0, wwa, g2, k_k, k_a, r_k, ln_w, ln_b):
    m = x2.shape[0]
    tb = MIX_BLOCK
    nblk = m // tb
    bpr = seq // tb
    c, d, hh = RET_CHUNK, RET_HEAD_DIM, RET_HEADS
    t, n, w = RWKV_CHUNK, RWKV_HEAD_DIM, RWKV_WIDTH
    cos, sin, mask, kte, qfs, cd = _retention_tables(seq)
    ones = jnp.full((d, d), 1.0 / d, BF16)
    grp = jnp.arange(w // 2) // n
    gmat = (grp[:, None] == grp[None, :]).astype(BF16)
    ti = jnp.arange(tb)
    tri = ((ti[:, None] >= ti[None, :])
           & (ti[:, None] // t == ti[None, :] // t)).astype(BF16)
    vec = _const_spec((1, w))
    tab_spec = pl.BlockSpec((tb, d), lambda i: ((i + bpr - 1) % bpr, 0))
    return pl.pallas_call(
        functools.partial(_mixer_kernel, bpr),
        grid=(nblk + 1,),
        in_specs=[
            pl.BlockSpec((tb, D_MODEL), lambda i: (jnp.minimum(i, nblk - 1), 0)),
            _const_spec((1, D_MODEL)),
            _const_spec((D_MODEL, IN_COLS)),
            tab_spec, tab_spec,
            _const_spec((hh, c, c)),
            _const_spec((hh, c, d)),
            _const_spec((hh, c, d)),
            _const_spec((hh, 1, d)),
            _const_spec((1, RET_WIDTH)),
            _const_spec((d, d)),
            _const_spec((1, RWKV_COLS)),
            vec, vec,
            _const_spec((DECAY_LORA + AAA_LORA, 2 * w)),
            _const_spec((GATE_LORA, w)),
            vec, vec, vec, vec, vec,
            _const_spec((w // 2, w // 2)),
            _const_spec((tb, tb)),
        ],
        out_specs=pl.BlockSpec((tb, D_MODEL), lambda i: (jnp.maximum(i - 1, 0), 0)),
        out_shape=jax.ShapeDtypeStruct((m, D_MODEL), BF16),
        scratch_shapes=[
            pltpu.VMEM((tb, IN_COLS), F32),
            pltpu.VMEM((tb, IN_COLS), F32),
            pltpu.VMEM((hh, d, d), F32),
            pltpu.VMEM((RWKV_HEADS // 2, 2 * n, 2 * n), F32),
            pltpu.VMEM((1, RWKV_COLS), F32),
            pltpu.VMEM((tb, w), F32),
        ],
        compiler_params=pltpu.CompilerParams(
            dimension_semantics=("arbitrary",), vmem_limit_bytes=V7X_VMEM_LIMIT),
        name="mixer",
    )(x2, nw, w_in_b, cos, sin, mask, kte, qfs, cd, rnw, ones, mu, w0, a0, wwa, g2,
      k_k, k_a, r_k, ln_w, ln_b, gmat, tri)


def _tail_kernel(x_ref, mixed_ref, p_ref, wo_ref, nfw_ref, wg_ref, wu_ref,
                 wd_ref, npw_ref, wpg_ref, wpu_ref, fnw_ref, o_ref):
    x = x_ref[...] + _dot(mixed_ref[...], wo_ref[...])
    hf = _rms(x, nfw_ref[...]).astype(BF16)
    gate = _dot(hf, wg_ref[...])
    up = _dot(hf, wu_ref[...])
    act = (gate * _sigmoid(gate) * up).astype(BF16)
    x = x + _dot(act, wd_ref[...])
    hp = _rms(x, npw_ref[...]).astype(BF16)
    pg = _sigmoid(_dot(hp, wpg_ref[...]))
    x = x + _dot(p_ref[...].astype(BF16), wpu_ref[...]) * pg
    o_ref[...] = _rms(x, fnw_ref[...])


def _tail(x2, mixed, p2, w_o, nfw, w_gate, w_up, w_down, npw, w_pg, w_pu, fnw):
    m = x2.shape[0]
    tm = TOKEN_TILE
    vec = _const_spec((1, D_MODEL))
    return pl.pallas_call(
        _tail_kernel,
        grid=(m // tm,),
        in_specs=[
            pl.BlockSpec((tm, D_MODEL), lambda i: (i, 0)),
            pl.BlockSpec((tm, D_MODEL), lambda i: (i, 0)),
            pl.BlockSpec((tm, D_PLE), lambda i: (i, 0)),
            _const_spec((D_MODEL, D_MODEL)),
            vec,
            _const_spec((D_MODEL, D_FF)),
            _const_spec((D_MODEL, D_FF)),
            _const_spec((D_FF, D_MODEL)),
            vec,
            _const_spec((D_MODEL, D_MODEL)),
            _const_spec((D_PLE, D_MODEL)),
            vec,
        ],
        out_specs=pl.BlockSpec((tm, D_MODEL), lambda i: (i, 0)),
        out_shape=jax.ShapeDtypeStruct((m, D_MODEL), F32),
        compiler_params=pltpu.CompilerParams(
            dimension_semantics=("arbitrary",), vmem_limit_bytes=V7X_VMEM_LIMIT),
        name="tail",
    )(x2, mixed, p2, w_o, nfw, w_gate, w_up, w_down, npw, w_pg, w_pu, fnw)


def kernel(x, p, norm_mix_w, w_in, ret_norm_w, rw_mu, rw_w0, rw_w2, rw_a0, rw_a2,
           rw_g2, rw_k_k, rw_k_a, rw_r_k, rw_ln_w, rw_ln_b, w_o, norm_ffn_w,
           w_gate, w_up, w_down, norm_ple_w, w_ple_gate, w_ple_up, final_norm_w):
    b, s, dm = x.shape
    assert w_in.shape[0] == 1, "single-layer trunk only"
    assert s % MIX_BLOCK == 0 and (b * s) % TOKEN_TILE == 0
    x2 = x.reshape(b * s, dm)
    row = lambda a: a.reshape(1, -1)
    w_in_b = w_in[0].astype(BF16)
    w_in_b = jnp.concatenate([_even_odd_columns(w_in_b[:, :2 * RET_WIDTH]),
                              w_in_b[:, 2 * RET_WIDTH:]], axis=1)
    zeros = jnp.zeros((DECAY_LORA, RWKV_WIDTH), F32)
    wwa = jnp.concatenate([
        jnp.concatenate([rw_w2[0], zeros], axis=1),
        jnp.concatenate([zeros, rw_a2[0]], axis=1)], axis=0).astype(BF16)
    mixed = _mixer(x2, s, row(norm_mix_w[0]), w_in_b, row(ret_norm_w[0]), row(rw_mu[0]),
                   row(rw_w0[0]), row(rw_a0[0]), wwa, rw_g2[0].astype(BF16),
                   row(rw_k_k[0]), row(rw_k_a[0]), row(rw_r_k[0]), row(rw_ln_w[0]),
                   row(rw_ln_b[0]))
    out = _tail(x2, mixed, p[0].reshape(b * s, D_PLE), w_o[0].astype(BF16),
                row(norm_ffn_w[0]), w_gate[0].astype(BF16), w_up[0].astype(BF16),
                w_down[0].astype(BF16), row(norm_ple_w[0]), w_ple_gate[0].astype(BF16),
                w_ple_up[0].astype(BF16), row(final_norm_w))
    return out.reshape(b, s, dm)
```

```python
import functools

import jax
import jax.numpy as jnp
import numpy as np
from jax import lax
from jax.experimental import pallas as pl
from jax.experimental.pallas import tpu as pltpu

F32 = jnp.float32
BF16 = jnp.bfloat16

D_MODEL = 1024
D_PLE = 256
RET_HEAD_DIM = 128
RET_WIDTH = 512
RET_HEADS = 4
RET_CHUNK = 128
RET_COLS = 4 * RET_WIDTH
RWKV_HEAD_DIM = 64
RWKV_WIDTH = 512
RWKV_HEADS = 8
DECAY_LORA = 64
AAA_LORA = 64
GATE_LORA = 128
RWKV_COLS = 3 * RWKV_WIDTH + DECAY_LORA + AAA_LORA + GATE_LORA
IN_COLS = RET_COLS + RWKV_COLS
D_FF = 2816
NORM_EPS = 1e-6
RET_GN_EPS = 1e-5
RWKV_GN_EPS = 64e-5

RWKV_CHUNK = 64
MIX_BLOCK = 256
PROJ_CHUNK = 256
TOKEN_TILE = 512
TAIL_GROUP = 256
V7X_VMEM_LIMIT = 56 * 1024 * 1024


def _dot(a, b):
    return jnp.dot(a, b, preferred_element_type=F32)


def _dot_nt(a, b):
    return lax.dot_general(a, b, (((1,), (1,)), ((), ())), preferred_element_type=F32)


def _dot_tn(a, b):
    return lax.dot_general(a, b, (((0,), (0,)), ((), ())), preferred_element_type=F32)


def _rms(x, w):
    ms = jnp.mean(x * x, axis=-1, keepdims=True)
    return x * lax.rsqrt(ms + NORM_EPS) * w


def _sigmoid(x):
    return 0.5 * jnp.tanh(0.5 * x) + 0.5


def _const_spec(shape):
    nd = len(shape)
    return pl.BlockSpec(shape, lambda *_: (0,) * nd, pipeline_mode=pl.Buffered(1))


def _retention_block(load, cos, sin, mask_ref, kte_ref, qfs_ref, cd_ref, nw_ref, ones,
                     state_ref, fill):
    c, d = RET_CHUNK, RET_HEAD_DIM
    nchunk = MIX_BLOCK // c

    def rot(t):
        return t * cos + pltpu.roll(t, d // 2, 1) * sin

    tiles = [(ci, h) for ci in range(nchunk) for h in range(RET_HEADS)]
    q, k, vb, gate = {}, {}, {}, {}
    for h in range(RET_HEADS):
        lo = h * d
        qh = rot(load(lo, lo + d))
        kh = rot(load(RET_WIDTH + lo, RET_WIDTH + lo + d))
        vh = load(2 * RET_WIDTH + lo, 2 * RET_WIDTH + lo + d).astype(BF16)
        gh = load(3 * RET_WIDTH + lo, 3 * RET_WIDTH + lo + d)
        gh = gh * _sigmoid(gh) * nw_ref[:, lo:lo + d]
        for ci in range(nchunk):
            rows = slice(ci * c, (ci + 1) * c)
            q[ci, h], k[ci, h], vb[ci, h], gate[ci, h] = qh[rows], kh[rows], vh[rows], gh[rows]
    fill(2)
    lhs, kd = {}, {}
    for cq in tiles:
        h = cq[1]
        scores = _dot_nt(q[cq].astype(BF16), k[cq].astype(BF16)) * mask_ref[h]
        lhs[cq] = jnp.concatenate([scores, q[cq] * qfs_ref[h]], axis=1).astype(BF16)
        kd[cq] = (k[cq] * kte_ref[h]).astype(BF16)
    r_cur = [state_ref[h] for h in range(RET_HEADS)]
    ys = []
    for ci in range(nchunk):
        for h in range(RET_HEADS):
            rhs = jnp.concatenate([vb[ci, h], r_cur[h].astype(BF16)], axis=0)
            ys.append(_dot(lhs[ci, h], rhs))
            r_cur[h] = r_cur[h] * cd_ref[h] + _dot_tn(kd[ci, h], vb[ci, h])
    for h in range(RET_HEADS):
        state_ref[h] = r_cur[h]

    fill(1)
    y = jnp.concatenate(ys, axis=0)
    dlt = y - _dot(y.astype(BF16), ones)
    var = _dot((dlt * dlt).astype(BF16), ones)
    yn = dlt * lax.rsqrt(var + RET_GN_EPS)
    return tiles, [yn[i * c:(i + 1) * c] * gate[cq] for i, cq in enumerate(tiles)]


def _retention_tables(seq):
    d, c, hh = RET_HEAD_DIM, RET_CHUNK, RET_HEADS
    pos = jnp.arange(seq, dtype=F32)
    angle = 1.0 / (10000.0 ** jnp.linspace(0.0, 1.0, d // 2, dtype=F32))
    phase = pos[:, None] * angle[None, :]
    cos = jnp.concatenate([jnp.cos(phase), jnp.cos(phase)], axis=1)
    sin = jnp.concatenate([-jnp.sin(phase), jnp.sin(phase)], axis=1)
    log_g = jnp.log(1.0 - 2.0 ** (-5.0 - jnp.arange(hh, dtype=F32)))
    idx = jnp.arange(c, dtype=F32)
    rel = idx[:, None] - idx[None, :]
    scale = d ** -0.5
    mask = jnp.where(rel[None] >= 0,
                     jnp.exp(log_g[:, None, None] * jnp.maximum(rel, 0.0)[None]), 0.0) * scale
    kte = jnp.exp(log_g[:, None] * (c - 1 - idx)[None]) * scale
    kte = jnp.broadcast_to(kte[:, :, None], (hh, c, d))
    qfs = jnp.exp(log_g[:, None] * (idx + 1.0)[None])
    qfs = jnp.broadcast_to(qfs[:, :, None], (hh, c, d))
    cd = jnp.broadcast_to(jnp.exp(log_g * c)[:, None, None], (hh, 1, d))
    return cos, sin, mask, kte, qfs, cd


def _even_odd_columns(w_cols):
    rows = w_cols.shape[0]
    w4 = w_cols.reshape(rows, -1, RET_HEAD_DIM // 2, 2)
    return jnp.swapaxes(w4, 2, 3).reshape(rows, -1)


def _split2(x):
    hi = x.astype(BF16)
    mid = (x - hi.astype(F32)).astype(BF16)
    return hi, mid


def _rwkv_block(h, mu_ref, w0_ref, a0_ref, wwa_ref, g2_ref, kk_ref, ka_ref, rk_ref,
                lnw_ref, lnb_ref, gsum_ref, tri_ref, state_ref, carry_ref, y_ref, fill):
    t = RWKV_CHUNK
    n = RWKV_HEAD_DIM
    w = RWKV_WIDTH
    tb = MIX_BLOCK
    pw = 2 * n

    row = lax.broadcasted_iota(jnp.int32, h.shape, 0)
    prev = jnp.where(row == 0, carry_ref[...], pltpu.roll(h, 1, 0))
    carry_ref[...] = h[tb - 1:tb, :]
    hs = h + (prev - h) * mu_ref[...]
    fill(1)

    r = hs[:, 0:w]
    k = hs[:, w:2 * w]
    v = hs[:, 2 * w:3 * w]
    wa = hs[:, 3 * w:3 * w + DECAY_LORA + AAA_LORA]
    gd = hs[:, 3 * w + DECAY_LORA + AAA_LORA:]
    lane = lax.broadcasted_iota(jnp.int32, wa.shape, 1)
    wa = jnp.where(lane < DECAY_LORA, jnp.tanh(wa), wa)
    lora = _dot(wa.astype(BF16), wwa_ref[...])
    wp = w0_ref[...] + lora[:, :w]
    logw = (-float(np.exp(-0.5))) * _sigmoid(wp)
    a = _sigmoid(a0_ref[...] + lora[:, w:])
    g = _dot(_sigmoid(gd).astype(BF16), g2_ref[...])
    fill(1)

    gmat = gsum_ref[...]
    half = w // 2

    def gsum(xs):
        stack = jnp.concatenate(
            [x[:, i * half:(i + 1) * half] for x in xs for i in range(2)], axis=0)
        res = _dot(stack.astype(BF16), gmat)
        return [jnp.concatenate([res[2 * j * tb:(2 * j + 1) * tb],
                                 res[(2 * j + 1) * tb:(2 * j + 2) * tb]], axis=1)
                for j in range(len(xs))]

    kk = k * kk_ref[...]
    k2 = k * (1.0 + (a - 1.0) * ka_ref[...])
    kk_sq, rk_sum = gsum([kk * kk, r * k2 * rk_ref[...]])
    kk = kk * lax.rsqrt(jnp.maximum(kk_sq, 1e-24))
    bonus = rk_sum * v
    bb = kk * a
    fill(1)

    tri = tri_ref[...]
    l_hi, l_mid = _split2(logw)
    cum = _dot(tri, l_hi) + _dot(tri, l_mid)
    fill(1)
    cum_end = jnp.concatenate(
        [jnp.broadcast_to(cum[(c + 1) * t - 1:(c + 1) * t, :], (t, w))
         for c in range(tb // t)], axis=0)
    e_neg = jnp.exp(-cum)
    a_f = -kk * jnp.exp(cum - logw)
    r_f = r * jnp.exp(cum)
    a_t = a_f.astype(BF16)
    r_t = r_f.astype(BF16)
    b_h = bb * e_neg
    k_h = k2 * e_neg
    w_end = jnp.exp(cum_end)
    to_end = w_end * e_neg
    b_w = bb * to_end
    k_w = k2 * to_end
    fill(1)

    ri = lax.broadcasted_iota(jnp.int32, (t, pw), 0)
    ci = lax.broadcasted_iota(jnp.int32, (t, pw), 1)
    cm = jnp.bitwise_and(ci, t - 1)
    incl = ri >= cm
    strict = ri > cm
    eye = jnp.where(ri == cm, 1.0, 0.0)
    lane0 = ci < n
    si = lax.broadcasted_iota(jnp.int32, (pw, pw), 0)
    sj = lax.broadcasted_iota(jnp.int32, (pw, pw), 1)
    same_head = (si < n) == (sj < n)

    def bd(x):
        return jnp.concatenate(
            [jnp.where(lane0, x, 0.0), jnp.where(lane0, 0.0, x)], axis=0).astype(BF16)

    nq = RWKV_HEADS // 2
    tiles = [(c, q) for c in range(tb // t) for q in range(nq)]

    def tile(arr, c, q):
        return arr[c * t:(c + 1) * t, q * pw:(q + 1) * pw]

    sc = {}
    for cq in tiles:
        ar = jnp.concatenate([tile(a_t, *cq), tile(r_t, *cq)], axis=0)
        bk = jnp.concatenate([bd(tile(b_h, *cq)), bd(tile(k_h, *cq))], axis=0)
        sc[cq] = _dot_nt(ar, bk)
    lm, m_rb, x, p = {}, {}, {}, {}
    for cq in tiles:
        l_ab = jnp.where(strict, sc[cq][0:t, 0:pw], 0.0)
        lm[cq] = jnp.concatenate(
            [jnp.where(strict, sc[cq][0:t, pw:], 0.0),
             jnp.where(incl, sc[cq][t:, pw:], 0.0)], axis=0).astype(BF16)
        m_rb[cq] = jnp.where(incl, sc[cq][t:, 0:pw], 0.0).astype(BF16)
        x[cq] = eye + l_ab
        p[cq] = _dot(l_ab.astype(BF16), bd(l_ab))
    for _ in range(int(np.log2(t)) - 2):
        for cq in tiles:
            z = _dot(jnp.concatenate([p[cq], x[cq]], axis=0).astype(BF16), bd(p[cq]))
            p[cq] = z[0:t]
            x[cq] = x[cq] + z[t:]
    for cq in tiles:
        x[cq] = (x[cq] + _dot(x[cq].astype(BF16), bd(p[cq]))).astype(BF16)

    lvov, xac = {}, {}
    for cq in tiles:
        lvov[cq] = _dot(lm[cq], bd(tile(v, *cq)))
    for cq in tiles:
        rhs = jnp.concatenate([bd(tile(a_f, *cq)), bd(lvov[cq][0:t])], axis=1)
        xac[cq] = _dot(x[cq], rhs)
    mq, nn, oc, wcb = {}, {}, {}, {}
    zero_tile = jnp.zeros((t, pw), F32)
    for cq in tiles:
        bkt = jnp.transpose(
            jnp.concatenate([tile(b_w, *cq), tile(k_w, *cq)], axis=0)).astype(BF16)
        low = jnp.concatenate([zero_tile, tile(v, *cq)], axis=1)
        mn = _dot(bkt, jnp.concatenate([xac[cq], low], axis=0).astype(BF16))
        qo = _dot(m_rb[cq], jnp.concatenate(
            [bd(xac[cq][:, 0:pw]), bd(xac[cq][:, pw:])], axis=1))
        mq[cq] = jnp.concatenate(
            [jnp.where(same_head, mn[:, 0:pw], 0.0),
             tile(r_f, *cq) + qo[:, 0:pw]], axis=0).astype(BF16)
        nn[cq] = jnp.where(same_head, mn[:, pw:], 0.0)
        oc[cq] = qo[:, pw:] + lvov[cq][t:]
        c, q = cq
        wcb[cq] = jnp.transpose(jnp.broadcast_to(
            w_end[c * t:c * t + 1, q * pw:(q + 1) * pw], (pw, pw)))

    fill(1)
    h_cur = [state_ref[q] for q in range(nq)]
    for c in range(tb // t):
        for q in range(nq):
            mh = _dot(mq[c, q], h_cur[q].astype(BF16))
            y_ref[c * t:(c + 1) * t, q * pw:(q + 1) * pw] = mh[pw:] + oc[c, q]
            h_cur[q] = wcb[c, q] * h_cur[q] + mh[0:pw] + nn[c, q]
        fill(1)
    for q in range(nq):
        state_ref[q] = h_cur[q]

    fill(2)
    y = y_ref[...]
    d = y - gsum([y])[0] * (1.0 / n)
    var = gsum([d * d])[0] * (1.0 / n)
    yn = d * lax.rsqrt(var + RWKV_GN_EPS) * lnw_ref[...] + lnb_ref[...]
    return (yn + bonus) * g


def _mixer_kernel(blocks_per_row,
                  x_ref, nw_ref, win_ref, cos_ref, sin_ref, mask_ref, kte_ref, qfs_ref,
                  cd_ref, rnw_ref, ones_ref, mu_ref, w0_ref, a0_ref, wwa_ref, g2_ref,
                  kk_ref, ka_ref, rk_ref, lnw_ref, lnb_ref, gsum_ref, tri_ref,
                  o_ref, proj_a_ref, proj_b_ref, ret_state_ref, rw_state_ref, carry_ref,
                  y_ref):
    i = pl.program_id(0)
    d = RET_HEAD_DIM

    @pl.when(i == 0)
    def _():
        proj_b_ref[...] = jnp.zeros_like(proj_b_ref)

    @pl.when(jnp.logical_or(i == 0, (i + blocks_per_row - 1) % blocks_per_row == 0))
    def _():
        ret_state_ref[...] = jnp.zeros_like(ret_state_ref)
        rw_state_ref[...] = jnp.zeros_like(rw_state_ref)
        carry_ref[...] = jnp.zeros_like(carry_ref)

    def step(new_ref, mix_ref):
        hn = _rms(x_ref[...], nw_ref[...]).astype(BF16)
        cols = iter(range(0, IN_COLS, PROJ_CHUNK))

        def fill(count):
            for _ in range(count):
                lo = next(cols, None)
                if lo is not None:
                    new_ref[:, lo:lo + PROJ_CHUNK] = _dot(hn, win_ref[:, lo:lo + PROJ_CHUNK])

        y_rw = _rwkv_block(mix_ref[:, RET_COLS:], mu_ref, w0_ref, a0_ref,
                           wwa_ref, g2_ref, kk_ref, ka_ref, rk_ref, lnw_ref, lnb_ref,
                           gsum_ref, tri_ref, rw_state_ref, carry_ref, y_ref, fill)
        o_ref[:, RET_WIDTH:] = y_rw.astype(BF16)

        tiles, y_ret = _retention_block(
            lambda lo, hi: mix_ref[:, lo:hi], cos_ref[...], sin_ref[...],
            mask_ref, kte_ref, qfs_ref, cd_ref, rnw_ref, ones_ref[...], ret_state_ref, fill)
        for (ci, h), yt in zip(tiles, y_ret):
            o_ref[ci * RET_CHUNK:(ci + 1) * RET_CHUNK, h * d:(h + 1) * d] = yt.astype(BF16)
        fill(IN_COLS // PROJ_CHUNK)

    pl.when(i % 2 == 0)(functools.partial(step, proj_a_ref, proj_b_ref))
    pl.when(i % 2 == 1)(functools.partial(step, proj_b_ref, proj_a_ref))


def _mixer(x2, seq, nw, w_in_b, rnw, mu, w0, a0, wwa, g2, k_k, k_a, r_k, ln_w, ln_b):
    m = x2.shape[0]
    tb = MIX_BLOCK
    nblk = m // tb
    bpr = seq // tb
    c, d, hh = RET_CHUNK, RET_HEAD_DIM, RET_HEADS
    t, n, w = RWKV_CHUNK, RWKV_HEAD_DIM, RWKV_WIDTH
    cos, sin, mask, kte, qfs, cd = _retention_tables(seq)
    ones = jnp.full((d, d), 1.0 / d, BF16)
    grp = jnp.arange(w // 2) // n
    gmat = (grp[:, None] == grp[None, :]).astype(BF16)
    ti = jnp.arange(tb)
    tri = ((ti[:, None] >= ti[None, :])
           & (ti[:, None] // t == ti[None, :] // t)).astype(BF16)
    vec = _const_spec((1, w))
    tab_spec = pl.BlockSpec((tb, d), lambda i: ((i + bpr - 1) % bpr, 0))
    return pl.pallas_call(
        functools.partial(_mixer_kernel, bpr),
        grid=(nblk + 1,),
        in_specs=[
            pl.BlockSpec((tb, D_MODEL), lambda i: (jnp.minimum(i, nblk - 1), 0)),
            _const_spec((1, D_MODEL)),
            _const_spec((D_MODEL, IN_COLS)),
            tab_spec, tab_spec,
            _const_spec((hh, c, c)),
            _const_spec((hh, c, d)),
            _const_spec((hh, c, d)),
            _const_spec((hh, 1, d)),
            _const_spec((1, RET_WIDTH)),
            _const_spec((d, d)),
            _const_spec((1, RWKV_COLS)),
            vec, vec,
            _const_spec((DECAY_LORA + AAA_LORA, 2 * w)),
            _const_spec((GATE_LORA, w)),
            vec, vec, vec, vec, vec,
            _const_spec((w // 2, w // 2)),
            _const_spec((tb, tb)),
        ],
        out_specs=pl.BlockSpec((tb, D_MODEL), lambda i: (jnp.maximum(i - 1, 0), 0)),
        out_shape=jax.ShapeDtypeStruct((m, D_MODEL), BF16),
        scratch_shapes=[
            pltpu.VMEM((tb, IN_COLS), F32),
            pltpu.VMEM((tb, IN_COLS), F32),
            pltpu.VMEM((hh, d, d), F32),
            pltpu.VMEM((RWKV_HEADS // 2, 2 * n, 2 * n), F32),
            pltpu.VMEM((1, RWKV_COLS), F32),
            pltpu.VMEM((tb, w), F32),
        ],
        compiler_params=pltpu.CompilerParams(
            dimension_semantics=("arbitrary",), vmem_limit_bytes=V7X_VMEM_LIMIT),
        name="mixer",
    )(x2, nw, w_in_b, cos, sin, mask, kte, qfs, cd, rnw, ones, mu, w0, a0, wwa, g2,
      k_k, k_a, r_k, ln_w, ln_b, gmat, tri)


def _tail_kernel(x_ref, mixed_ref, p_ref, wo_ref, nfw_ref, wg_ref, wu_ref,
                 wd_ref, npw_ref, wpg_ref, wpu_ref, fnw_ref, o_ref):
    rows = [slice(g * TAIL_GROUP, (g + 1) * TAIL_GROUP) for g in range(TOKEN_TILE // TAIL_GROUP)]
    xs = [x_ref[r] + _dot(mixed_ref[r], wo_ref[...]) for r in rows]
    hfs = [_rms(x, nfw_ref[...]).astype(BF16) for x in xs]
    acts = []
    for hf in hfs:
        gate = _dot(hf, wg_ref[...])
        up = _dot(hf, wu_ref[...])
        acts.append((gate * _sigmoid(gate) * up).astype(BF16))
    xs = [x + _dot(act, wd_ref[...]) for x, act in zip(xs, acts)]
    hps = [_rms(x, npw_ref[...]).astype(BF16) for x in xs]
    pgs = [_sigmoid(_dot(hp, wpg_ref[...])) for hp in hps]
    for r, x, pg in zip(rows, xs, pgs):
        x = x + _dot(p_ref[r].astype(BF16), wpu_ref[...]) * pg
        o_ref[r] = _rms(x, fnw_ref[...])


def _tail(x2, mixed, p2, w_o, nfw, w_gate, w_up, w_down, npw, w_pg, w_pu, fnw):
    m = x2.shape[0]
    tm = TOKEN_TILE
    vec = _const_spec((1, D_MODEL))
    return pl.pallas_call(
        _tail_kernel,
        grid=(m // tm,),
        in_specs=[
            pl.BlockSpec((tm, D_MODEL), lambda i: (i, 0)),
            pl.BlockSpec((tm, D_MODEL), lambda i: (i, 0)),
            pl.BlockSpec((tm, D_PLE), lambda i: (i, 0)),
            _const_spec((D_MODEL, D_MODEL)),
            vec,
            _const_spec((D_MODEL, D_FF)),
            _const_spec((D_MODEL, D_FF)),
            _const_spec((D_FF, D_MODEL)),
            vec,
            _const_spec((D_MODEL, D_MODEL)),
            _const_spec((D_PLE, D_MODEL)),
            vec,
        ],
        out_specs=pl.BlockSpec((tm, D_MODEL), lambda i: (i, 0)),
        out_shape=jax.ShapeDtypeStruct((m, D_MODEL), F32),
        compiler_params=pltpu.CompilerParams(
            dimension_semantics=("arbitrary",), vmem_limit_bytes=V7X_VMEM_LIMIT),
        name="tail",
    )(x2, mixed, p2, w_o, nfw, w_gate, w_up, w_down, npw, w_pg, w_pu, fnw)


def kernel(x, p, norm_mix_w, w_in, ret_norm_w, rw_mu, rw_w0, rw_w2, rw_a0, rw_a2,
           rw_g2, rw_k_k, rw_k_a, rw_r_k, rw_ln_w, rw_ln_b, w_o, norm_ffn_w,
           w_gate, w_up, w_down, norm_ple_w, w_ple_gate, w_ple_up, final_norm_w):
    b, s, dm = x.shape
    assert w_in.shape[0] == 1, "single-layer trunk only"
    assert s % MIX_BLOCK == 0 and (b * s) % TOKEN_TILE == 0
    x2 = x.reshape(b * s, dm)
    row = lambda a: a.reshape(1, -1)
    w_in_b = w_in[0].astype(BF16)
    w_in_b = jnp.concatenate([_even_odd_columns(w_in_b[:, :2 * RET_WIDTH]),
                              w_in_b[:, 2 * RET_WIDTH:]], axis=1)
    zeros = jnp.zeros((DECAY_LORA, RWKV_WIDTH), F32)
    wwa = jnp.concatenate([
        jnp.concatenate([rw_w2[0], zeros], axis=1),
        jnp.concatenate([zeros, rw_a2[0]], axis=1)], axis=0).astype(BF16)
    mixed = _mixer(x2, s, row(norm_mix_w[0]), w_in_b, row(ret_norm_w[0]), row(rw_mu[0]),
                   row(rw_w0[0]), row(rw_a0[0]), wwa, rw_g2[0].astype(BF16),
                   row(rw_k_k[0]), row(rw_k_a[0]), row(rw_r_k[0]), row(rw_ln_w[0]),
                   row(rw_ln_b[0]))
    out = _tail(x2, mixed, p[0].reshape(b * s, D_PLE), w_o[0].astype(BF16),
                row(norm_ffn_w[0]), w_gate[0].astype(BF16), w_up[0].astype(BF16),
                w_down[0].astype(BF16), row(norm_ple_w[0]), w_ple_gate[0].astype(BF16),
                w_ple_up[0].astype(BF16), row(final_norm_w))
    return out.reshape(b, s, dm)
```

```python
import functools

import jax
import jax.numpy as jnp
import numpy as np
from jax import lax
from jax.experimental import pallas as pl
from jax.experimental.pallas import tpu as pltpu

F32 = jnp.float32
BF16 = jnp.bfloat16

D_MODEL = 1024
D_PLE = 256
RET_HEAD_DIM = 128
RET_WIDTH = 512
RET_HEADS = 4
RET_CHUNK = 128
RET_COLS = 4 * RET_WIDTH
RWKV_HEAD_DIM = 64
RWKV_WIDTH = 512
RWKV_HEADS = 8
DECAY_LORA = 64
AAA_LORA = 64
GATE_LORA = 128
RWKV_COLS = 3 * RWKV_WIDTH + DECAY_LORA + AAA_LORA + GATE_LORA
IN_COLS = RET_COLS + RWKV_COLS
D_FF = 2816
NORM_EPS = 1e-6
RET_GN_EPS = 1e-5
RWKV_GN_EPS = 64e-5

RWKV_CHUNK = 64
MIX_BLOCK = 256
PROJ_CHUNK = 256
TOKEN_TILE = 512
TAIL_GROUP = 256
MIX_LIVE_ARRAYS = 16
V7X_VMEM_BYTES = 64 * 1024 * 1024


def _nbytes(shape, dtype):
    return int(np.prod(shape)) * np.dtype(dtype).itemsize


def _vmem_limit(single_buffered, double_buffered, live_values):
    total = sum(single_buffered) + 2 * sum(double_buffered) + sum(live_values)
    assert total <= V7X_VMEM_BYTES, total
    return total


def _dot(a, b):
    return jnp.dot(a, b, preferred_element_type=F32)


def _dot_nt(a, b):
    return lax.dot_general(a, b, (((1,), (1,)), ((), ())), preferred_element_type=F32)


def _dot_tn(a, b):
    return lax.dot_general(a, b, (((0,), (0,)), ((), ())), preferred_element_type=F32)


def _rms(x, w):
    ms = jnp.mean(x * x, axis=-1, keepdims=True)
    return x * lax.rsqrt(ms + NORM_EPS) * w


def _sigmoid(x):
    return 0.5 * jnp.tanh(0.5 * x) + 0.5


def _const_spec(shape):
    nd = len(shape)
    return pl.BlockSpec(shape, lambda *_: (0,) * nd, pipeline_mode=pl.Buffered(1))


def _retention_block(load, cos, sin, mask_ref, kte_ref, qfs_ref, cd_ref, nw_ref, ones,
                     state_ref, fill):
    c, d = RET_CHUNK, RET_HEAD_DIM
    nchunk = MIX_BLOCK // c

    def rot(t):
        return t * cos + pltpu.roll(t, d // 2, 1) * sin

    tiles = [(ci, h) for ci in range(nchunk) for h in range(RET_HEADS)]
    q, k, vb, gate = {}, {}, {}, {}
    for h in range(RET_HEADS):
        lo = h * d
        qh = rot(load(lo, lo + d))
        kh = rot(load(RET_WIDTH + lo, RET_WIDTH + lo + d))
        vh = load(2 * RET_WIDTH + lo, 2 * RET_WIDTH + lo + d).astype(BF16)
        gh = load(3 * RET_WIDTH + lo, 3 * RET_WIDTH + lo + d)
        gh = gh * _sigmoid(gh) * nw_ref[:, lo:lo + d]
        for ci in range(nchunk):
            rows = slice(ci * c, (ci + 1) * c)
            q[ci, h], k[ci, h], vb[ci, h], gate[ci, h] = qh[rows], kh[rows], vh[rows], gh[rows]
    fill(2)
    lhs, kd = {}, {}
    for cq in tiles:
        h = cq[1]
        scores = _dot_nt(q[cq].astype(BF16), k[cq].astype(BF16)) * mask_ref[h]
        lhs[cq] = jnp.concatenate([scores, q[cq] * qfs_ref[h]], axis=1).astype(BF16)
        kd[cq] = (k[cq] * kte_ref[h]).astype(BF16)
    r_cur = [state_ref[h] for h in range(RET_HEADS)]
    ys = []
    for ci in range(nchunk):
        for h in range(RET_HEADS):
            rhs = jnp.concatenate([vb[ci, h], r_cur[h].astype(BF16)], axis=0)
            ys.append(_dot(lhs[ci, h], rhs))
            r_cur[h] = r_cur[h] * cd_ref[h] + _dot_tn(kd[ci, h], vb[ci, h])
    for h in range(RET_HEADS):
        state_ref[h] = r_cur[h]

    fill(1)
    y = jnp.concatenate(ys, axis=0)
    dlt = y - _dot(y.astype(BF16), ones)
    var = _dot((dlt * dlt).astype(BF16), ones)
    yn = dlt * lax.rsqrt(var + RET_GN_EPS)
    return tiles, [yn[i * c:(i + 1) * c] * gate[cq] for i, cq in enumerate(tiles)]


def _retention_tables(seq):
    d, c, hh = RET_HEAD_DIM, RET_CHUNK, RET_HEADS
    pos = jnp.arange(seq, dtype=F32)
    angle = 1.0 / (10000.0 ** jnp.linspace(0.0, 1.0, d // 2, dtype=F32))
    phase = pos[:, None] * angle[None, :]
    cos = jnp.concatenate([jnp.cos(phase), jnp.cos(phase)], axis=1)
    sin = jnp.concatenate([-jnp.sin(phase), jnp.sin(phase)], axis=1)
    log_g = jnp.log(1.0 - 2.0 ** (-5.0 - jnp.arange(hh, dtype=F32)))
    idx = jnp.arange(c, dtype=F32)
    rel = idx[:, None] - idx[None, :]
    scale = d ** -0.5
    mask = jnp.where(rel[None] >= 0,
                     jnp.exp(log_g[:, None, None] * jnp.maximum(rel, 0.0)[None]), 0.0) * scale
    kte = jnp.exp(log_g[:, None] * (c - 1 - idx)[None]) * scale
    kte = jnp.broadcast_to(kte[:, :, None], (hh, c, d))
    qfs = jnp.exp(log_g[:, None] * (idx + 1.0)[None])
    qfs = jnp.broadcast_to(qfs[:, :, None], (hh, c, d))
    cd = jnp.broadcast_to(jnp.exp(log_g * c)[:, None, None], (hh, 1, d))
    return cos, sin, mask, kte, qfs, cd


def _even_odd_columns(w_cols):
    rows = w_cols.shape[0]
    w4 = w_cols.reshape(rows, -1, RET_HEAD_DIM // 2, 2)
    return jnp.swapaxes(w4, 2, 3).reshape(rows, -1)


def _split2(x):
    hi = x.astype(BF16)
    mid = (x - hi.astype(F32)).astype(BF16)
    return hi, mid


def _rwkv_block(h, mu_ref, w0_ref, a0_ref, wwa_ref, g2_ref, kk_ref, ka_ref, rk_ref,
                lnw_ref, lnb_ref, gsum_ref, tri_ref, state_ref, carry_ref, y_ref, fill):
    t = RWKV_CHUNK
    n = RWKV_HEAD_DIM
    w = RWKV_WIDTH
    tb = MIX_BLOCK
    pw = 2 * n

    row = lax.broadcasted_iota(jnp.int32, h.shape, 0)
    prev = jnp.where(row == 0, carry_ref[...], pltpu.roll(h, 1, 0))
    carry_ref[...] = h[tb - 1:tb, :]
    hs = h + (prev - h) * mu_ref[...]
    fill(1)

    r = hs[:, 0:w]
    k = hs[:, w:2 * w]
    v = hs[:, 2 * w:3 * w]
    wa = hs[:, 3 * w:3 * w + DECAY_LORA + AAA_LORA]
    gd = hs[:, 3 * w + DECAY_LORA + AAA_LORA:]
    lane = lax.broadcasted_iota(jnp.int32, wa.shape, 1)
    wa = jnp.where(lane < DECAY_LORA, jnp.tanh(wa), wa)
    lora = _dot(wa.astype(BF16), wwa_ref[...])
    wp = w0_ref[...] + lora[:, :w]
    logw = (-float(np.exp(-0.5))) * _sigmoid(wp)
    a = _sigmoid(a0_ref[...] + lora[:, w:])
    g = _dot(_sigmoid(gd).astype(BF16), g2_ref[...])
    fill(1)

    gmat = gsum_ref[...]
    half = w // 2

    def gsum(xs):
        stack = jnp.concatenate(
            [x[:, i * half:(i + 1) * half] for x in xs for i in range(2)], axis=0)
        res = _dot(stack.astype(BF16), gmat)
        return [jnp.concatenate([res[2 * j * tb:(2 * j + 1) * tb],
                                 res[(2 * j + 1) * tb:(2 * j + 2) * tb]], axis=1)
                for j in range(len(xs))]

    kk = k * kk_ref[...]
    k2 = k * (1.0 + (a - 1.0) * ka_ref[...])
    kk_sq, rk_sum = gsum([kk * kk, r * k2 * rk_ref[...]])
    kk = kk * lax.rsqrt(jnp.maximum(kk_sq, 1e-24))
    bonus = rk_sum * v
    bb = kk * a
    fill(1)

    tri = tri_ref[...]
    l_hi, l_mid = _split2(logw)
    cum = _dot(tri, l_hi) + _dot(tri, l_mid)
    fill(1)
    cum_end = jnp.concatenate(
        [jnp.broadcast_to(cum[(c + 1) * t - 1:(c + 1) * t, :], (t, w))
         for c in range(tb // t)], axis=0)
    e_neg = jnp.exp(-cum)
    a_f = -kk * jnp.exp(cum - logw)
    r_f = r * jnp.exp(cum)
    a_t = a_f.astype(BF16)
    r_t = r_f.astype(BF16)
    b_h = bb * e_neg
    k_h = k2 * e_neg
    w_end = jnp.exp(cum_end)
    to_end = w_end * e_neg
    b_w = bb * to_end
    k_w = k2 * to_end
    fill(1)

    ri = lax.broadcasted_iota(jnp.int32, (t, pw), 0)
    ci = lax.broadcasted_iota(jnp.int32, (t, pw), 1)
    cm = jnp.bitwise_and(ci, t - 1)
    incl = ri >= cm
    strict = ri > cm
    eye = jnp.where(ri == cm, 1.0, 0.0)
    lane0 = ci < n
    si = lax.broadcasted_iota(jnp.int32, (pw, pw), 0)
    sj = lax.broadcasted_iota(jnp.int32, (pw, pw), 1)
    same_head = (si < n) == (sj < n)

    def bd(x):
        xb = x.astype(BF16)
        zb = jnp.zeros_like(xb)
        return jnp.concatenate([jnp.where(lane0, xb, zb), jnp.where(lane0, zb, xb)], axis=0)

    nq = RWKV_HEADS // 2
    tiles = [(c, q) for c in range(tb // t) for q in range(nq)]

    def tile(arr, c, q):
        return arr[c * t:(c + 1) * t, q * pw:(q + 1) * pw]

    sc = {}
    for cq in tiles:
        ar = jnp.concatenate([tile(a_t, *cq), tile(r_t, *cq)], axis=0)
        bk = jnp.concatenate([bd(tile(b_h, *cq)), bd(tile(k_h, *cq))], axis=0)
        sc[cq] = _dot_nt(ar, bk)
    lm, m_rb, x, p = {}, {}, {}, {}
    for cq in tiles:
        l_ab = jnp.where(strict, sc[cq][0:t, 0:pw], 0.0)
        lm[cq] = jnp.concatenate(
            [jnp.where(strict, sc[cq][0:t, pw:], 0.0),
             jnp.where(incl, sc[cq][t:, pw:], 0.0)], axis=0).astype(BF16)
        m_rb[cq] = jnp.where(incl, sc[cq][t:, 0:pw], 0.0).astype(BF16)
        x[cq] = eye + l_ab
        p[cq] = _dot(l_ab.astype(BF16), bd(l_ab))
    for _ in range(int(np.log2(t)) - 2):
        for cq in tiles:
            z = _dot(jnp.concatenate([p[cq], x[cq]], axis=0).astype(BF16), bd(p[cq]))
            p[cq] = z[0:t]
            x[cq] = x[cq] + z[t:]
    for cq in tiles:
        x[cq] = (x[cq] + _dot(x[cq].astype(BF16), bd(p[cq]))).astype(BF16)

    lvov, xac = {}, {}
    for cq in tiles:
        lvov[cq] = _dot(lm[cq], bd(tile(v, *cq)))
    for cq in tiles:
        rhs = jnp.concatenate([bd(tile(a_f, *cq)), bd(lvov[cq][0:t])], axis=1)
        xac[cq] = _dot(x[cq], rhs)
    mq, nn, oc, wcb = {}, {}, {}, {}
    zero_tile = jnp.zeros((t, pw), F32)
    for cq in tiles:
        bkt = jnp.transpose(
            jnp.concatenate([tile(b_w, *cq), tile(k_w, *cq)], axis=0)).astype(BF16)
        low = jnp.concatenate([zero_tile, tile(v, *cq)], axis=1)
        mn = _dot(bkt, jnp.concatenate([xac[cq], low], axis=0).astype(BF16))
        qo = _dot(m_rb[cq], jnp.concatenate(
            [bd(xac[cq][:, 0:pw]), bd(xac[cq][:, pw:])], axis=1))
        mq[cq] = jnp.concatenate(
            [jnp.where(same_head, mn[:, 0:pw], 0.0),
             tile(r_f, *cq) + qo[:, 0:pw]], axis=0).astype(BF16)
        nn[cq] = jnp.where(same_head, mn[:, pw:], 0.0)
        oc[cq] = qo[:, pw:] + lvov[cq][t:]
        c, q = cq
        wcb[cq] = jnp.transpose(jnp.broadcast_to(
            w_end[c * t:c * t + 1, q * pw:(q + 1) * pw], (pw, pw)))

    fill(1)
    h_cur = [state_ref[q] for q in range(nq)]
    for c in range(tb // t):
        for q in range(nq):
            mh = _dot(mq[c, q], h_cur[q].astype(BF16))
            y_ref[c * t:(c + 1) * t, q * pw:(q + 1) * pw] = mh[pw:] + oc[c, q]
            h_cur[q] = wcb[c, q] * h_cur[q] + mh[0:pw] + nn[c, q]
        fill(1)
    for q in range(nq):
        state_ref[q] = h_cur[q]

    fill(2)
    y = y_ref[...]
    d = y - gsum([y])[0] * (1.0 / n)
    var = gsum([d * d])[0] * (1.0 / n)
    yn = d * lax.rsqrt(var + RWKV_GN_EPS) * lnw_ref[...] + lnb_ref[...]
    return (yn + bonus) * g


def _mixer_kernel(blocks_per_row,
                  x_ref, nw_ref, win_ref, cos_ref, sin_ref, mask_ref, kte_ref, qfs_ref,
                  cd_ref, rnw_ref, ones_ref, mu_ref, w0_ref, a0_ref, wwa_ref, g2_ref,
                  kk_ref, ka_ref, rk_ref, lnw_ref, lnb_ref, gsum_ref, tri_ref,
                  o_ref, proj_a_ref, proj_b_ref, ret_state_ref, rw_state_ref, carry_ref,
                  y_ref):
    i = pl.program_id(0)
    d = RET_HEAD_DIM

    @pl.when(i == 0)
    def _():
        proj_b_ref[...] = jnp.zeros_like(proj_b_ref)

    @pl.when(jnp.logical_or(i == 0, (i + blocks_per_row - 1) % blocks_per_row == 0))
    def _():
        ret_state_ref[...] = jnp.zeros_like(ret_state_ref)
        rw_state_ref[...] = jnp.zeros_like(rw_state_ref)
        carry_ref[...] = jnp.zeros_like(carry_ref)

    def step(new_ref, mix_ref):
        hn = _rms(x_ref[...], nw_ref[...]).astype(BF16)
        cols = iter(range(0, IN_COLS, PROJ_CHUNK))

        def fill(count):
            for _ in range(count):
                lo = next(cols, None)
                if lo is not None:
                    new_ref[:, lo:lo + PROJ_CHUNK] = _dot(hn, win_ref[:, lo:lo + PROJ_CHUNK])

        y_rw = _rwkv_block(mix_ref[:, RET_COLS:], mu_ref, w0_ref, a0_ref,
                           wwa_ref, g2_ref, kk_ref, ka_ref, rk_ref, lnw_ref, lnb_ref,
                           gsum_ref, tri_ref, rw_state_ref, carry_ref, y_ref, fill)
        o_ref[:, RET_WIDTH:] = y_rw.astype(BF16)

        tiles, y_ret = _retention_block(
            lambda lo, hi: mix_ref[:, lo:hi], cos_ref[...], sin_ref[...],
            mask_ref, kte_ref, qfs_ref, cd_ref, rnw_ref, ones_ref[...], ret_state_ref, fill)
        for (ci, h), yt in zip(tiles, y_ret):
            o_ref[ci * RET_CHUNK:(ci + 1) * RET_CHUNK, h * d:(h + 1) * d] = yt.astype(BF16)
        fill(IN_COLS // PROJ_CHUNK)

    pl.when(i % 2 == 0)(functools.partial(step, proj_a_ref, proj_b_ref))
    pl.when(i % 2 == 1)(functools.partial(step, proj_b_ref, proj_a_ref))


def _mixer(x2, seq, nw, w_in_b, rnw, mu, w0, a0, wwa, g2, k_k, k_a, r_k, ln_w, ln_b):
    m = x2.shape[0]
    tb = MIX_BLOCK
    nblk = m // tb
    bpr = seq // tb
    c, d, hh = RET_CHUNK, RET_HEAD_DIM, RET_HEADS
    t, n, w = RWKV_CHUNK, RWKV_HEAD_DIM, RWKV_WIDTH
    cos, sin, mask, kte, qfs, cd = _retention_tables(seq)
    ones = jnp.full((d, d), 1.0 / d, BF16)
    grp = jnp.arange(w // 2) // n
    gmat = (grp[:, None] == grp[None, :]).astype(BF16)
    ti = jnp.arange(tb)
    tri = ((ti[:, None] >= ti[None, :])
           & (ti[:, None] // t == ti[None, :] // t)).astype(BF16)
    vec = _const_spec((1, w))
    tab_spec = pl.BlockSpec((tb, d), lambda i: ((i + bpr - 1) % bpr, 0))
    constants = (nw, w_in_b, mask, kte, qfs, cd, rnw, ones, mu, w0, a0, wwa, g2,
                 k_k, k_a, r_k, ln_w, ln_b, gmat, tri)
    scratch = [
        pltpu.VMEM((tb, IN_COLS), F32),
        pltpu.VMEM((tb, IN_COLS), F32),
        pltpu.VMEM((hh, d, d), F32),
        pltpu.VMEM((RWKV_HEADS // 2, 2 * n, 2 * n), F32),
        pltpu.VMEM((1, RWKV_COLS), F32),
        pltpu.VMEM((tb, w), F32),
    ]
    vmem_limit = _vmem_limit(
        [_nbytes(a.shape, a.dtype) for a in constants]
        + [_nbytes(sc.shape, sc.dtype) for sc in scratch],
        [_nbytes((tb, D_MODEL), F32), _nbytes((tb, D_MODEL), BF16),
         2 * _nbytes((tb, d), F32)],
        [_nbytes((tb, 2 * RWKV_COLS + RET_COLS + MIX_LIVE_ARRAYS * w), F32)])
    return pl.pallas_call(
        functools.partial(_mixer_kernel, bpr),
        grid=(nblk + 1,),
        in_specs=[
            pl.BlockSpec((tb, D_MODEL), lambda i: (jnp.minimum(i, nblk - 1), 0)),
            _const_spec((1, D_MODEL)),
            _const_spec((D_MODEL, IN_COLS)),
            tab_spec, tab_spec,
            _const_spec((hh, c, c)),
            _const_spec((hh, c, d)),
            _const_spec((hh, c, d)),
            _const_spec((hh, 1, d)),
            _const_spec((1, RET_WIDTH)),
            _const_spec((d, d)),
            _const_spec((1, RWKV_COLS)),
            vec, vec,
            _const_spec((DECAY_LORA + AAA_LORA, 2 * w)),
            _const_spec((GATE_LORA, w)),
            vec, vec, vec, vec, vec,
            _const_spec((w // 2, w // 2)),
            _const_spec((tb, tb)),
        ],
        out_specs=pl.BlockSpec((tb, D_MODEL), lambda i: (jnp.maximum(i - 1, 0), 0)),
        out_shape=jax.ShapeDtypeStruct((m, D_MODEL), BF16),
        scratch_shapes=scratch,
        compiler_params=pltpu.CompilerParams(
            dimension_semantics=("arbitrary",), vmem_limit_bytes=vmem_limit),
        name="mixer",
    )(x2, nw, w_in_b, cos, sin, mask, kte, qfs, cd, rnw, ones, mu, w0, a0, wwa, g2,
      k_k, k_a, r_k, ln_w, ln_b, gmat, tri)


def _tail_kernel(x_ref, mixed_ref, p_ref, wo_ref, nfw_ref, wg_ref, wu_ref,
                 wd_ref, npw_ref, wpg_ref, wpu_ref, fnw_ref, o_ref):
    rows = [slice(g * TAIL_GROUP, (g + 1) * TAIL_GROUP) for g in range(TOKEN_TILE // TAIL_GROUP)]
    xs = [x_ref[r] + _dot(mixed_ref[r], wo_ref[...]) for r in rows]
    hfs = [_rms(x, nfw_ref[...]).astype(BF16) for x in xs]
    acts = []
    for hf in hfs:
        gate = _dot(hf, wg_ref[...])
        up = _dot(hf, wu_ref[...])
        acts.append((gate * _sigmoid(gate) * up).astype(BF16))
    xs = [x + _dot(act, wd_ref[...]) for x, act in zip(xs, acts)]
    hps = [_rms(x, npw_ref[...]).astype(BF16) for x in xs]
    pgs = [_sigmoid(_dot(hp, wpg_ref[...])) for hp in hps]
    for r, x, pg in zip(rows, xs, pgs):
        x = x + _dot(p_ref[r].astype(BF16), wpu_ref[...]) * pg
        o_ref[r] = _rms(x, fnw_ref[...])


def _tail(x2, mixed, p2, w_o, nfw, w_gate, w_up, w_down, npw, w_pg, w_pu, fnw):
    m = x2.shape[0]
    tm = TOKEN_TILE
    vec = _const_spec((1, D_MODEL))
    groups = tm // TAIL_GROUP
    vmem_limit = _vmem_limit(
        [_nbytes(a.shape, a.dtype)
         for a in (w_o, nfw, w_gate, w_up, w_down, npw, w_pg, w_pu, fnw)],
        [_nbytes((tm, D_MODEL), F32), _nbytes((tm, D_MODEL), BF16),
         _nbytes((tm, D_PLE), F32), _nbytes((tm, D_MODEL), F32)],
        [groups * (_nbytes((TAIL_GROUP, D_MODEL + 2 * D_FF), F32)
                   + _nbytes((TAIL_GROUP, D_FF), BF16))])
    return pl.pallas_call(
        _tail_kernel,
        grid=(m // tm,),
        in_specs=[
            pl.BlockSpec((tm, D_MODEL), lambda i: (i, 0)),
            pl.BlockSpec((tm, D_MODEL), lambda i: (i, 0)),
            pl.BlockSpec((tm, D_PLE), lambda i: (i, 0)),
            _const_spec((D_MODEL, D_MODEL)),
            vec,
            _const_spec((D_MODEL, D_FF)),
            _const_spec((D_MODEL, D_FF)),
            _const_spec((D_FF, D_MODEL)),
            vec,
            _const_spec((D_MODEL, D_MODEL)),
            _const_spec((D_PLE, D_MODEL)),
            vec,
        ],
        out_specs=pl.BlockSpec((tm, D_MODEL), lambda i: (i, 0)),
        out_shape=jax.ShapeDtypeStruct((m, D_MODEL), F32),
        compiler_params=pltpu.CompilerParams(
            dimension_semantics=("arbitrary",), vmem_limit_bytes=vmem_limit),
        name="tail",
    )(x2, mixed, p2, w_o, nfw, w_gate, w_up, w_down, npw, w_pg, w_pu, fnw)


def kernel(x, p, norm_mix_w, w_in, ret_norm_w, rw_mu, rw_w0, rw_w2, rw_a0, rw_a2,
           rw_g2, rw_k_k, rw_k_a, rw_r_k, rw_ln_w, rw_ln_b, w_o, norm_ffn_w,
           w_gate, w_up, w_down, norm_ple_w, w_ple_gate, w_ple_up, final_norm_w):
    b, s, dm = x.shape
    assert w_in.shape[0] == 1, "single-layer trunk only"
    assert s % MIX_BLOCK == 0 and (b * s) % TOKEN_TILE == 0
    x2 = x.reshape(b * s, dm)
    row = lambda a: a.reshape(1, -1)
    w_in_b = w_in[0].astype(BF16)
    w_in_b = jnp.concatenate([_even_odd_columns(w_in_b[:, :2 * RET_WIDTH]),
                              w_in_b[:, 2 * RET_WIDTH:]], axis=1)
    zeros = jnp.zeros((DECAY_LORA, RWKV_WIDTH), F32)
    wwa = jnp.concatenate([
        jnp.concatenate([rw_w2[0], zeros], axis=1),
        jnp.concatenate([zeros, rw_a2[0]], axis=1)], axis=0).astype(BF16)
    mixed = _mixer(x2, s, row(norm_mix_w[0]), w_in_b, row(ret_norm_w[0]), row(rw_mu[0]),
                   row(rw_w0[0]), row(rw_a0[0]), wwa, rw_g2[0].astype(BF16),
                   row(rw_k_k[0]), row(rw_k_a[0]), row(rw_r_k[0]), row(rw_ln_w[0]),
                   row(rw_ln_b[0]))
    out = _tail(x2, mixed, p[0].reshape(b * s, D_PLE), w_o[0].astype(BF16),
                row(norm_ffn_w[0]), w_gate[0].astype(BF16), w_up[0].astype(BF16),
                w_down[0].astype(BF16), row(norm_ple_w[0]), w_ple_gate[0].astype(BF16),
                w_ple_up[0].astype(BF16), row(final_norm_w))
    return out.reshape(b, s, dm)
```

```python
import functools

import jax
import jax.numpy as jnp
import numpy as np
from jax import lax
from jax.experimental import pallas as pl
from jax.experimental.pallas import tpu as pltpu

F32 = jnp.float32
BF16 = jnp.bfloat16

D_MODEL = 1024
D_PLE = 256
RET_HEAD_DIM = 128
RET_WIDTH = 512
RET_HEADS = 4
RET_CHUNK = 128
RET_COLS = 4 * RET_WIDTH
RWKV_HEAD_DIM = 64
RWKV_WIDTH = 512
RWKV_HEADS = 8
DECAY_LORA = 64
AAA_LORA = 64
GATE_LORA = 128
RWKV_COLS = 3 * RWKV_WIDTH + DECAY_LORA + AAA_LORA + GATE_LORA
IN_COLS = RET_COLS + RWKV_COLS
D_FF = 2816
NORM_EPS = 1e-6
RET_GN_EPS = 1e-5
RWKV_GN_EPS = 64e-5

RWKV_CHUNK = 64
MIX_BLOCK = 256
PROJ_CHUNK = 256
TOKEN_TILE = 512
TAIL_GROUP = 256
MIX_LIVE_ARRAYS = 16
V7X_VMEM_BYTES = 64 * 1024 * 1024


def _nbytes(shape, dtype):
    return int(np.prod(shape)) * np.dtype(dtype).itemsize


def _vmem_limit(single_buffered, double_buffered, live_values):
    total = sum(single_buffered) + 2 * sum(double_buffered) + sum(live_values)
    assert total <= V7X_VMEM_BYTES, total
    return total


def _dot(a, b):
    return jnp.dot(a, b, preferred_element_type=F32)


def _dot_nt(a, b):
    return lax.dot_general(a, b, (((1,), (1,)), ((), ())), preferred_element_type=F32)


def _dot_tn(a, b):
    return lax.dot_general(a, b, (((0,), (0,)), ((), ())), preferred_element_type=F32)


def _rms(x, w):
    ms = jnp.mean(x * x, axis=-1, keepdims=True)
    return x * lax.rsqrt(ms + NORM_EPS) * w


def _sigmoid(x):
    return 0.5 * jnp.tanh(0.5 * x) + 0.5


def _const_spec(shape):
    nd = len(shape)
    return pl.BlockSpec(shape, lambda *_: (0,) * nd, pipeline_mode=pl.Buffered(1))


def _retention_block(load, cos, sin, mask_ref, kte_ref, qfs_ref, cd_ref, nw_ref, ones,
                     state_ref, fill):
    c, d = RET_CHUNK, RET_HEAD_DIM
    nchunk = MIX_BLOCK // c

    def rot(t):
        return t * cos + pltpu.roll(t, d // 2, 1) * sin

    tiles = [(ci, h) for ci in range(nchunk) for h in range(RET_HEADS)]
    q, k, vb, gate = {}, {}, {}, {}
    for h in range(RET_HEADS):
        lo = h * d
        qh = rot(load(lo, lo + d))
        kh = rot(load(RET_WIDTH + lo, RET_WIDTH + lo + d))
        vh = load(2 * RET_WIDTH + lo, 2 * RET_WIDTH + lo + d).astype(BF16)
        gh = load(3 * RET_WIDTH + lo, 3 * RET_WIDTH + lo + d)
        gh = gh * _sigmoid(gh) * nw_ref[:, lo:lo + d]
        for ci in range(nchunk):
            rows = slice(ci * c, (ci + 1) * c)
            q[ci, h], k[ci, h], vb[ci, h], gate[ci, h] = qh[rows], kh[rows], vh[rows], gh[rows]
    fill(2)
    lhs, kd = {}, {}
    for cq in tiles:
        h = cq[1]
        scores = _dot_nt(q[cq].astype(BF16), k[cq].astype(BF16)) * mask_ref[h]
        lhs[cq] = jnp.concatenate([scores, q[cq] * qfs_ref[h]], axis=1).astype(BF16)
        kd[cq] = (k[cq] * kte_ref[h]).astype(BF16)
    r_cur = [state_ref[h] for h in range(RET_HEADS)]
    ys = []
    for ci in range(nchunk):
        for h in range(RET_HEADS):
            rhs = jnp.concatenate([vb[ci, h], r_cur[h].astype(BF16)], axis=0)
            ys.append(_dot(lhs[ci, h], rhs))
            r_cur[h] = r_cur[h] * cd_ref[h] + _dot_tn(kd[ci, h], vb[ci, h])
    for h in range(RET_HEADS):
        state_ref[h] = r_cur[h]

    fill(1)
    y = jnp.concatenate(ys, axis=0)
    dlt = y - _dot(y.astype(BF16), ones)
    var = _dot((dlt * dlt).astype(BF16), ones)
    yn = dlt * lax.rsqrt(var + RET_GN_EPS)
    return tiles, [yn[i * c:(i + 1) * c] * gate[cq] for i, cq in enumerate(tiles)]


def _retention_tables(seq):
    d, c, hh = RET_HEAD_DIM, RET_CHUNK, RET_HEADS
    pos = jnp.arange(seq, dtype=F32)
    angle = 1.0 / (10000.0 ** jnp.linspace(0.0, 1.0, d // 2, dtype=F32))
    phase = pos[:, None] * angle[None, :]
    cos = jnp.concatenate([jnp.cos(phase), jnp.cos(phase)], axis=1)
    sin = jnp.concatenate([-jnp.sin(phase), jnp.sin(phase)], axis=1)
    log_g = jnp.log(1.0 - 2.0 ** (-5.0 - jnp.arange(hh, dtype=F32)))
    idx = jnp.arange(c, dtype=F32)
    rel = idx[:, None] - idx[None, :]
    scale = d ** -0.5
    mask = jnp.where(rel[None] >= 0,
                     jnp.exp(log_g[:, None, None] * jnp.maximum(rel, 0.0)[None]), 0.0) * scale
    kte = jnp.exp(log_g[:, None] * (c - 1 - idx)[None]) * scale
    kte = jnp.broadcast_to(kte[:, :, None], (hh, c, d))
    qfs = jnp.exp(log_g[:, None] * (idx + 1.0)[None])
    qfs = jnp.broadcast_to(qfs[:, :, None], (hh, c, d))
    cd = jnp.broadcast_to(jnp.exp(log_g * c)[:, None, None], (hh, 1, d))
    return cos, sin, mask, kte, qfs, cd


def _even_odd_columns(w_cols):
    rows = w_cols.shape[0]
    w4 = w_cols.reshape(rows, -1, RET_HEAD_DIM // 2, 2)
    return jnp.swapaxes(w4, 2, 3).reshape(rows, -1)


def _split2(x):
    hi = x.astype(BF16)
    mid = (x - hi.astype(F32)).astype(BF16)
    return hi, mid


def _rwkv_block(h, mu_ref, w0_ref, a0_ref, wwa_ref, g2_ref, kk_ref, ka_ref, rk_ref,
                lnw_ref, lnb_ref, gsum_ref, tri_ref, state_ref, carry_ref, y_ref, fill):
    t = RWKV_CHUNK
    n = RWKV_HEAD_DIM
    w = RWKV_WIDTH
    tb = MIX_BLOCK
    pw = 2 * n

    row = lax.broadcasted_iota(jnp.int32, h.shape, 0)
    prev = jnp.where(row == 0, carry_ref[...], pltpu.roll(h, 1, 0))
    carry_ref[...] = h[tb - 1:tb, :]
    hs = h + (prev - h) * mu_ref[...]
    fill(1)

    r = hs[:, 0:w]
    k = hs[:, w:2 * w]
    v = hs[:, 2 * w:3 * w]
    wa = hs[:, 3 * w:3 * w + DECAY_LORA + AAA_LORA]
    gd = hs[:, 3 * w + DECAY_LORA + AAA_LORA:]
    lane = lax.broadcasted_iota(jnp.int32, wa.shape, 1)
    wa = jnp.where(lane < DECAY_LORA, jnp.tanh(wa), wa)
    lora = _dot(wa.astype(BF16), wwa_ref[...])
    wp = w0_ref[...] + lora[:, :w]
    logw = (-float(np.exp(-0.5))) * _sigmoid(wp)
    a = _sigmoid(a0_ref[...] + lora[:, w:])
    g = _dot(_sigmoid(gd).astype(BF16), g2_ref[...])
    fill(1)

    gmat = gsum_ref[...]
    half = w // 2

    def gsum(xs):
        stack = jnp.concatenate(
            [x[:, i * half:(i + 1) * half] for x in xs for i in range(2)], axis=0)
        res = _dot(stack.astype(BF16), gmat)
        return [jnp.concatenate([res[2 * j * tb:(2 * j + 1) * tb],
                                 res[(2 * j + 1) * tb:(2 * j + 2) * tb]], axis=1)
                for j in range(len(xs))]

    kk = k * kk_ref[...]
    k2 = k * (1.0 + (a - 1.0) * ka_ref[...])
    kk_sq, rk_sum = gsum([kk * kk, r * k2 * rk_ref[...]])
    kk = kk * lax.rsqrt(jnp.maximum(kk_sq, 1e-24))
    bonus = rk_sum * v
    bb = kk * a
    fill(1)

    tri = tri_ref[...]
    l_hi, l_mid = _split2(logw)
    cum = _dot(tri, l_hi) + _dot(tri, l_mid)
    fill(1)
    cum_end = jnp.concatenate(
        [jnp.broadcast_to(cum[(c + 1) * t - 1:(c + 1) * t, :], (t, w))
         for c in range(tb // t)], axis=0)
    e_neg = jnp.exp(-cum)
    a_f = -kk * jnp.exp(cum - logw)
    r_f = r * jnp.exp(cum)
    a_t = a_f.astype(BF16)
    r_t = r_f.astype(BF16)
    b_h = bb * e_neg
    k_h = k2 * e_neg
    w_end = jnp.exp(cum_end)
    to_end = w_end * e_neg
    b_w = bb * to_end
    k_w = k2 * to_end
    fill(1)

    ri = lax.broadcasted_iota(jnp.int32, (t, pw), 0)
    ci = lax.broadcasted_iota(jnp.int32, (t, pw), 1)
    cm = jnp.bitwise_and(ci, t - 1)
    incl = ri >= cm
    strict = ri > cm
    eye = jnp.where(ri == cm, 1.0, 0.0)
    lane0 = ci < n
    si = lax.broadcasted_iota(jnp.int32, (pw, pw), 0)
    sj = lax.broadcasted_iota(jnp.int32, (pw, pw), 1)
    same_head = (si < n) == (sj < n)

    def bd(x):
        return jnp.concatenate(
            [jnp.where(lane0, x, 0.0), jnp.where(lane0, 0.0, x)], axis=0).astype(BF16)

    nq = RWKV_HEADS // 2
    tiles = [(c, q) for c in range(tb // t) for q in range(nq)]

    def tile(arr, c, q):
        return arr[c * t:(c + 1) * t, q * pw:(q + 1) * pw]

    sc = {}
    for cq in tiles:
        ar = jnp.concatenate([tile(a_t, *cq), tile(r_t, *cq)], axis=0)
        bk = jnp.concatenate([bd(tile(b_h, *cq)), bd(tile(k_h, *cq))], axis=0)
        sc[cq] = _dot_nt(ar, bk)
    lm, m_rb, x, p = {}, {}, {}, {}
    for cq in tiles:
        l_ab = jnp.where(strict, sc[cq][0:t, 0:pw], 0.0)
        lm[cq] = jnp.concatenate(
            [jnp.where(strict, sc[cq][0:t, pw:], 0.0),
             jnp.where(incl, sc[cq][t:, pw:], 0.0)], axis=0).astype(BF16)
        m_rb[cq] = jnp.where(incl, sc[cq][t:, 0:pw], 0.0).astype(BF16)
        x[cq] = eye + l_ab
        p[cq] = _dot(l_ab.astype(BF16), bd(l_ab))
    for _ in range(int(np.log2(t)) - 2):
        for cq in tiles:
            z = _dot(jnp.concatenate([p[cq], x[cq]], axis=0).astype(BF16), bd(p[cq]))
            p[cq] = z[0:t]
            x[cq] = x[cq] + z[t:]
    for cq in tiles:
        x[cq] = (x[cq] + _dot(x[cq].astype(BF16), bd(p[cq]))).astype(BF16)

    lvov, xac = {}, {}
    for cq in tiles:
        lvov[cq] = _dot(lm[cq], bd(tile(v, *cq)))
    for cq in tiles:
        rhs = jnp.concatenate([bd(tile(a_f, *cq)), bd(lvov[cq][0:t])], axis=1)
        xac[cq] = _dot(x[cq], rhs)
    mq, nn, oc, wcb = {}, {}, {}, {}
    zero_tile = jnp.zeros((t, pw), F32)
    for cq in tiles:
        bkt = jnp.transpose(
            jnp.concatenate([tile(b_w, *cq), tile(k_w, *cq)], axis=0)).astype(BF16)
        low = jnp.concatenate([zero_tile, tile(v, *cq)], axis=1)
        mn = _dot(bkt, jnp.concatenate([xac[cq], low], axis=0).astype(BF16))
        qo = _dot(m_rb[cq], jnp.concatenate(
            [bd(xac[cq][:, 0:pw]), bd(xac[cq][:, pw:])], axis=1))
        mq[cq] = jnp.concatenate(
            [jnp.where(same_head, mn[:, 0:pw], 0.0),
             tile(r_f, *cq) + qo[:, 0:pw]], axis=0).astype(BF16)
        nn[cq] = jnp.where(same_head, mn[:, pw:], 0.0)
        oc[cq] = qo[:, pw:] + lvov[cq][t:]
        c, q = cq
        wcb[cq] = jnp.transpose(jnp.broadcast_to(
            w_end[c * t:c * t + 1, q * pw:(q + 1) * pw], (pw, pw)))

    fill(1)
    h_cur = [state_ref[q] for q in range(nq)]
    for c in range(tb // t):
        for q in range(nq):
            mh = _dot(mq[c, q], h_cur[q].astype(BF16))
            y_ref[c * t:(c + 1) * t, q * pw:(q + 1) * pw] = mh[pw:] + oc[c, q]
            h_cur[q] = wcb[c, q] * h_cur[q] + mh[0:pw] + nn[c, q]
        fill(1)
    for q in range(nq):
        state_ref[q] = h_cur[q]

    fill(2)
    y = y_ref[...]
    d = y - gsum([y])[0] * (1.0 / n)
    var = gsum([d * d])[0] * (1.0 / n)
    yn = d * lax.rsqrt(var + RWKV_GN_EPS) * lnw_ref[...] + lnb_ref[...]
    return (yn + bonus) * g


def _mixer_kernel(blocks_per_row,
                  x_ref, nw_ref, win_ref, cos_ref, sin_ref, mask_ref, kte_ref, qfs_ref,
                  cd_ref, rnw_ref, ones_ref, mu_ref, w0_ref, a0_ref, wwa_ref, g2_ref,
                  kk_ref, ka_ref, rk_ref, lnw_ref, lnb_ref, gsum_ref, tri_ref,
                  o_ref, proj_a_ref, proj_b_ref, ret_state_ref, rw_state_ref, carry_ref,
                  y_ref):
    i = pl.program_id(0)
    d = RET_HEAD_DIM

    @pl.when(i == 0)
    def _():
        proj_b_ref[...] = jnp.zeros_like(proj_b_ref)

    @pl.when(jnp.logical_or(i == 0, (i + blocks_per_row - 1) % blocks_per_row == 0))
    def _():
        ret_state_ref[...] = jnp.zeros_like(ret_state_ref)
        rw_state_ref[...] = jnp.zeros_like(rw_state_ref)
        carry_ref[...] = jnp.zeros_like(carry_ref)

    def step(new_ref, mix_ref):
        hn = _rms(x_ref[...], nw_ref[...]).astype(BF16)
        cols = iter(range(0, IN_COLS, PROJ_CHUNK))

        def fill(count):
            for _ in range(count):
                lo = next(cols, None)
                if lo is not None:
                    new_ref[:, lo:lo + PROJ_CHUNK] = _dot(hn, win_ref[:, lo:lo + PROJ_CHUNK])

        y_rw = _rwkv_block(mix_ref[:, RET_COLS:], mu_ref, w0_ref, a0_ref,
                           wwa_ref, g2_ref, kk_ref, ka_ref, rk_ref, lnw_ref, lnb_ref,
                           gsum_ref, tri_ref, rw_state_ref, carry_ref, y_ref, fill)
        o_ref[:, RET_WIDTH:] = y_rw.astype(BF16)

        tiles, y_ret = _retention_block(
            lambda lo, hi: mix_ref[:, lo:hi], cos_ref[...], sin_ref[...],
            mask_ref, kte_ref, qfs_ref, cd_ref, rnw_ref, ones_ref[...], ret_state_ref, fill)
        for (ci, h), yt in zip(tiles, y_ret):
            o_ref[ci * RET_CHUNK:(ci + 1) * RET_CHUNK, h * d:(h + 1) * d] = yt.astype(BF16)
        fill(IN_COLS // PROJ_CHUNK)

    pl.when(i % 2 == 0)(functools.partial(step, proj_a_ref, proj_b_ref))
    pl.when(i % 2 == 1)(functools.partial(step, proj_b_ref, proj_a_ref))


def _mixer(x2, seq, nw, w_in_b, rnw, mu, w0, a0, wwa, g2, k_k, k_a, r_k, ln_w, ln_b):
    m = x2.shape[0]
    tb = MIX_BLOCK
    nblk = m // tb
    bpr = seq // tb
    c, d, hh = RET_CHUNK, RET_HEAD_DIM, RET_HEADS
    t, n, w = RWKV_CHUNK, RWKV_HEAD_DIM, RWKV_WIDTH
    cos, sin, mask, kte, qfs, cd = _retention_tables(seq)
    ones = jnp.full((d, d), 1.0 / d, BF16)
    grp = jnp.arange(w // 2) // n
    gmat = (grp[:, None] == grp[None, :]).astype(BF16)
    ti = jnp.arange(tb)
    tri = ((ti[:, None] >= ti[None, :])
           & (ti[:, None] // t == ti[None, :] // t)).astype(BF16)
    vec = _const_spec((1, w))
    tab_spec = pl.BlockSpec((tb, d), lambda i: ((i + bpr - 1) % bpr, 0))
    constants = (nw, w_in_b, mask, kte, qfs, cd, rnw, ones, mu, w0, a0, wwa, g2,
                 k_k, k_a, r_k, ln_w, ln_b, gmat, tri)
    scratch = [
        pltpu.VMEM((tb, IN_COLS), F32),
        pltpu.VMEM((tb, IN_COLS), F32),
        pltpu.VMEM((hh, d, d), F32),
        pltpu.VMEM((RWKV_HEADS // 2, 2 * n, 2 * n), F32),
        pltpu.VMEM((1, RWKV_COLS), F32),
        pltpu.VMEM((tb, w), F32),
    ]
    vmem_limit = _vmem_limit(
        [_nbytes(a.shape, a.dtype) for a in constants]
        + [_nbytes(sc.shape, sc.dtype) for sc in scratch],
        [_nbytes((tb, D_MODEL), F32), _nbytes((tb, D_MODEL), BF16),
         2 * _nbytes((tb, d), F32)],
        [_nbytes((tb, 2 * RWKV_COLS + RET_COLS + MIX_LIVE_ARRAYS * w), F32)])
    return pl.pallas_call(
        functools.partial(_mixer_kernel, bpr),
        grid=(nblk + 1,),
        in_specs=[
            pl.BlockSpec((tb, D_MODEL), lambda i: (jnp.minimum(i, nblk - 1), 0)),
            _const_spec((1, D_MODEL)),
            _const_spec((D_MODEL, IN_COLS)),
            tab_spec, tab_spec,
            _const_spec((hh, c, c)),
            _const_spec((hh, c, d)),
            _const_spec((hh, c, d)),
            _const_spec((hh, 1, d)),
            _const_spec((1, RET_WIDTH)),
            _const_spec((d, d)),
            _const_spec((1, RWKV_COLS)),
            vec, vec,
            _const_spec((DECAY_LORA + AAA_LORA, 2 * w)),
            _const_spec((GATE_LORA, w)),
            vec, vec, vec, vec, vec,
            _const_spec((w // 2, w // 2)),
            _const_spec((tb, tb)),
        ],
        out_specs=pl.BlockSpec((tb, D_MODEL), lambda i: (jnp.maximum(i - 1, 0), 0)),
        out_shape=jax.ShapeDtypeStruct((m, D_MODEL), BF16),
        scratch_shapes=scratch,
        compiler_params=pltpu.CompilerParams(
            dimension_semantics=("arbitrary",), vmem_limit_bytes=vmem_limit),
        name="mixer",
    )(x2, nw, w_in_b, cos, sin, mask, kte, qfs, cd, rnw, ones, mu, w0, a0, wwa, g2,
      k_k, k_a, r_k, ln_w, ln_b, gmat, tri)


def _tail_kernel(x_ref, mixed_ref, p_ref, wo_ref, nfw_ref, wg_ref, wu_ref,
                 wd_ref, npw_ref, wpg_ref, wpu_ref, fnw_ref, o_ref):
    rows = [slice(g * TAIL_GROUP, (g + 1) * TAIL_GROUP) for g in range(TOKEN_TILE // TAIL_GROUP)]
    xs = [x_ref[r] + _dot(mixed_ref[r], wo_ref[...]) for r in rows]
    hfs = [_rms(x, nfw_ref[...]).astype(BF16) for x in xs]
    acts = []
    for hf in hfs:
        gate = _dot(hf, wg_ref[...])
        up = _dot(hf, wu_ref[...])
        acts.append((gate * _sigmoid(gate) * up).astype(BF16))
    xs = [x + _dot(act, wd_ref[...]) for x, act in zip(xs, acts)]
    hps = [_rms(x, npw_ref[...]).astype(BF16) for x in xs]
    pgs = [_sigmoid(_dot(hp, wpg_ref[...])) for hp in hps]
    for r, x, pg in zip(rows, xs, pgs):
        x = x + _dot(p_ref[r].astype(BF16), wpu_ref[...]) * pg
        o_ref[r] = _rms(x, fnw_ref[...])


def _tail(x2, mixed, p2, w_o, nfw, w_gate, w_up, w_down, npw, w_pg, w_pu, fnw):
    m = x2.shape[0]
    tm = TOKEN_TILE
    vec = _const_spec((1, D_MODEL))
    groups = tm // TAIL_GROUP
    vmem_limit = _vmem_limit(
        [_nbytes(a.shape, a.dtype)
         for a in (w_o, nfw, w_gate, w_up, w_down, npw, w_pg, w_pu, fnw)],
        [_nbytes((tm, D_MODEL), F32), _nbytes((tm, D_MODEL), BF16),
         _nbytes((tm, D_PLE), F32), _nbytes((tm, D_MODEL), F32)],
        [groups * (_nbytes((TAIL_GROUP, D_MODEL + 2 * D_FF), F32)
                   + _nbytes((TAIL_GROUP, D_FF), BF16))])
    return pl.pallas_call(
        _tail_kernel,
        grid=(m // tm,),
        in_specs=[
            pl.BlockSpec((tm, D_MODEL), lambda i: (i, 0)),
            pl.BlockSpec((tm, D_MODEL), lambda i: (i, 0)),
            pl.BlockSpec((tm, D_PLE), lambda i: (i, 0)),
            _const_spec((D_MODEL, D_MODEL)),
            vec,
            _const_spec((D_MODEL, D_FF)),
            _const_spec((D_MODEL, D_FF)),
            _const_spec((D_FF, D_MODEL)),
            vec,
            _const_spec((D_MODEL, D_MODEL)),
            _const_spec((D_PLE, D_MODEL)),
            vec,
        ],
        out_specs=pl.BlockSpec((tm, D_MODEL), lambda i: (i, 0)),
        out_shape=jax.ShapeDtypeStruct((m, D_MODEL), F32),
        compiler_params=pltpu.CompilerParams(
            dimension_semantics=("arbitrary",), vmem_limit_bytes=vmem_limit),
        name="tail",
    )(x2, mixed, p2, w_o, nfw, w_gate, w_up, w_down, npw, w_pg, w_pu, fnw)


def kernel(x, p, norm_mix_w, w_in, ret_norm_w, rw_mu, rw_w0, rw_w2, rw_a0, rw_a2,
           rw_g2, rw_k_k, rw_k_a, rw_r_k, rw_ln_w, rw_ln_b, w_o, norm_ffn_w,
           w_gate, w_up, w_down, norm_ple_w, w_ple_gate, w_ple_up, final_norm_w):
    b, s, dm = x.shape
    assert w_in.shape[0] == 1, "single-layer trunk only"
    assert s % MIX_BLOCK == 0 and (b * s) % TOKEN_TILE == 0
    x2 = x.reshape(b * s, dm)
    row = lambda a: a.reshape(1, -1)
    w_in_b = w_in[0].astype(BF16)
    w_in_b = jnp.concatenate([_even_odd_columns(w_in_b[:, :2 * RET_WIDTH]),
                              w_in_b[:, 2 * RET_WIDTH:]], axis=1)
    zeros = jnp.zeros((DECAY_LORA, RWKV_WIDTH), F32)
    wwa = jnp.concatenate([
        jnp.concatenate([rw_w2[0], zeros], axis=1),
        jnp.concatenate([zeros, rw_a2[0]], axis=1)], axis=0).astype(BF16)
    mixed = _mixer(x2, s, row(norm_mix_w[0]), w_in_b, row(ret_norm_w[0]), row(rw_mu[0]),
                   row(rw_w0[0]), row(rw_a0[0]), wwa, rw_g2[0].astype(BF16),
                   row(rw_k_k[0]), row(rw_k_a[0]), row(rw_r_k[0]), row(rw_ln_w[0]),
                   row(rw_ln_b[0]))
    out = _tail(x2, mixed, p[0].reshape(b * s, D_PLE), w_o[0].astype(BF16),
                row(norm_ffn_w[0]), w_gate[0].astype(BF16), w_up[0].astype(BF16),
                w_down[0].astype(BF16), row(norm_ple_w[0]), w_ple_gate[0].astype(BF16),
                w_ple_up[0].astype(BF16), row(final_norm_w))
    return out.reshape(b, s, dm)
```

```python
import functools

import jax
import jax.numpy as jnp
import numpy as np
from jax import lax
from jax.experimental import pallas as pl
from jax.experimental.pallas import tpu as pltpu

F32 = jnp.float32
BF16 = jnp.bfloat16

D_MODEL = 1024
D_PLE = 256
RET_HEAD_DIM = 128
RET_WIDTH = 512
RET_HEADS = 4
RET_CHUNK = 128
RET_COLS = 4 * RET_WIDTH
RWKV_HEAD_DIM = 64
RWKV_WIDTH = 512
RWKV_HEADS = 8
DECAY_LORA = 64
AAA_LORA = 64
GATE_LORA = 128
RWKV_COLS = 3 * RWKV_WIDTH + DECAY_LORA + AAA_LORA + GATE_LORA
IN_COLS = RET_COLS + RWKV_COLS
D_FF = 2816
NORM_EPS = 1e-6
RET_GN_EPS = 1e-5
RWKV_GN_EPS = 64e-5

RWKV_CHUNK = 64
MIX_BLOCK = 256
PROJ_CHUNK = 256
TOKEN_TILE = 512
TAIL_GROUP = 256
V7X_VMEM_LIMIT = 56 * 1024 * 1024


def _dot(a, b):
    return jnp.dot(a, b, preferred_element_type=F32)


def _dot_nt(a, b):
    return lax.dot_general(a, b, (((1,), (1,)), ((), ())), preferred_element_type=F32)


def _dot_tn(a, b):
    return lax.dot_general(a, b, (((0,), (0,)), ((), ())), preferred_element_type=F32)


def _rms(x, w):
    ms = jnp.mean(x * x, axis=-1, keepdims=True)
    return x * lax.rsqrt(ms + NORM_EPS) * w


def _sigmoid(x):
    return 0.5 * jnp.tanh(0.5 * x) + 0.5


def _const_spec(shape):
    nd = len(shape)
    return pl.BlockSpec(shape, lambda *_: (0,) * nd, pipeline_mode=pl.Buffered(1))


def _retention_block(load, cos, sin, mask_ref, kte_ref, qfs_ref, cd_ref, nw_ref, ones,
                     state_ref, fill):
    c, d = RET_CHUNK, RET_HEAD_DIM
    nchunk = MIX_BLOCK // c

    def rot(t):
        return t * cos + pltpu.roll(t, d // 2, 1) * sin

    tiles = [(ci, h) for ci in range(nchunk) for h in range(RET_HEADS)]
    q, k, vb, gate = {}, {}, {}, {}
    for h in range(RET_HEADS):
        lo = h * d
        qh = rot(load(lo, lo + d))
        kh = rot(load(RET_WIDTH + lo, RET_WIDTH + lo + d))
        vh = load(2 * RET_WIDTH + lo, 2 * RET_WIDTH + lo + d).astype(BF16)
        gh = load(3 * RET_WIDTH + lo, 3 * RET_WIDTH + lo + d)
        gh = gh * _sigmoid(gh) * nw_ref[:, lo:lo + d]
        for ci in range(nchunk):
            rows = slice(ci * c, (ci + 1) * c)
            q[ci, h], k[ci, h], vb[ci, h], gate[ci, h] = qh[rows], kh[rows], vh[rows], gh[rows]
    fill(2)
    lhs, kd = {}, {}
    for cq in tiles:
        h = cq[1]
        scores = _dot_nt(q[cq].astype(BF16), k[cq].astype(BF16)) * mask_ref[h]
        lhs[cq] = jnp.concatenate([scores, q[cq] * qfs_ref[h]], axis=1).astype(BF16)
        kd[cq] = (k[cq] * kte_ref[h]).astype(BF16)
    r_cur = [state_ref[h] for h in range(RET_HEADS)]
    ys = []
    for ci in range(nchunk):
        for h in range(RET_HEADS):
            rhs = jnp.concatenate([vb[ci, h], r_cur[h].astype(BF16)], axis=0)
            ys.append(_dot(lhs[ci, h], rhs))
            r_cur[h] = r_cur[h] * cd_ref[h] + _dot_tn(kd[ci, h], vb[ci, h])
    for h in range(RET_HEADS):
        state_ref[h] = r_cur[h]

    fill(1)
    y = jnp.concatenate(ys, axis=0)
    dlt = y - _dot(y.astype(BF16), ones)
    var = _dot((dlt * dlt).astype(BF16), ones)
    yn = dlt * lax.rsqrt(var + RET_GN_EPS)
    return tiles, [yn[i * c:(i + 1) * c] * gate[cq] for i, cq in enumerate(tiles)]


def _retention_tables(seq):
    d, c, hh = RET_HEAD_DIM, RET_CHUNK, RET_HEADS
    pos = jnp.arange(seq, dtype=F32)
    angle = 1.0 / (10000.0 ** jnp.linspace(0.0, 1.0, d // 2, dtype=F32))
    phase = pos[:, None] * angle[None, :]
    cos = jnp.concatenate([jnp.cos(phase), jnp.cos(phase)], axis=1)
    sin = jnp.concatenate([-jnp.sin(phase), jnp.sin(phase)], axis=1)
    log_g = jnp.log(1.0 - 2.0 ** (-5.0 - jnp.arange(hh, dtype=F32)))
    idx = jnp.arange(c, dtype=F32)
    rel = idx[:, None] - idx[None, :]
    scale = d ** -0.5
    mask = jnp.where(rel[None] >= 0,
                     jnp.exp(log_g[:, None, None] * jnp.maximum(rel, 0.0)[None]), 0.0) * scale
    kte = jnp.exp(log_g[:, None] * (c - 1 - idx)[None]) * scale
    kte = jnp.broadcast_to(kte[:, :, None], (hh, c, d))
    qfs = jnp.exp(log_g[:, None] * (idx + 1.0)[None])
    qfs = jnp.broadcast_to(qfs[:, :, None], (hh, c, d))
    cd = jnp.broadcast_to(jnp.exp(log_g * c)[:, None, None], (hh, 1, d))
    return cos, sin, mask, kte, qfs, cd


def _even_odd_columns(w_cols):
    rows = w_cols.shape[0]
    w4 = w_cols.reshape(rows, -1, RET_HEAD_DIM // 2, 2)
    return jnp.swapaxes(w4, 2, 3).reshape(rows, -1)


def _split2(x):
    hi = x.astype(BF16)
    mid = (x - hi.astype(F32)).astype(BF16)
    return hi, mid


def _rwkv_block(h, mu_ref, w0_ref, a0_ref, wwa_ref, g2_ref, kk_ref, ka_ref, rk_ref,
                lnw_ref, lnb_ref, gsum_ref, tri_ref, state_ref, carry_ref, y_ref, fill):
    t = RWKV_CHUNK
    n = RWKV_HEAD_DIM
    w = RWKV_WIDTH
    tb = MIX_BLOCK
    pw = 2 * n

    row = lax.broadcasted_iota(jnp.int32, h.shape, 0)
    prev = jnp.where(row == 0, carry_ref[...], pltpu.roll(h, 1, 0))
    carry_ref[...] = h[tb - 1:tb, :]
    hs = h + (prev - h) * mu_ref[...]
    fill(1)

    r = hs[:, 0:w]
    k = hs[:, w:2 * w]
    v = hs[:, 2 * w:3 * w]
    wa = hs[:, 3 * w:3 * w + DECAY_LORA + AAA_LORA]
    gd = hs[:, 3 * w + DECAY_LORA + AAA_LORA:]
    lane = lax.broadcasted_iota(jnp.int32, wa.shape, 1)
    wa = jnp.where(lane < DECAY_LORA, jnp.tanh(wa), wa)
    lora = _dot(wa.astype(BF16), wwa_ref[...])
    wp = w0_ref[...] + lora[:, :w]
    logw = (-float(np.exp(-0.5))) * _sigmoid(wp)
    a = _sigmoid(a0_ref[...] + lora[:, w:])
    g = _dot(_sigmoid(gd).astype(BF16), g2_ref[...])
    fill(1)

    gmat = gsum_ref[...]
    half = w // 2

    def gsum(xs):
        stack = jnp.concatenate(
            [x[:, i * half:(i + 1) * half] for x in xs for i in range(2)], axis=0)
        res = _dot(stack.astype(BF16), gmat)
        return [jnp.concatenate([res[2 * j * tb:(2 * j + 1) * tb],
                                 res[(2 * j + 1) * tb:(2 * j + 2) * tb]], axis=1)
                for j in range(len(xs))]

    kk = k * kk_ref[...]
    k2 = k * (1.0 + (a - 1.0) * ka_ref[...])
    kk_sq, rk_sum = gsum([kk * kk, r * k2 * rk_ref[...]])
    kk = kk * lax.rsqrt(jnp.maximum(kk_sq, 1e-24))
    bonus = rk_sum * v
    bb = kk * a
    fill(1)

    tri = tri_ref[...]
    l_hi, l_mid = _split2(logw)
    cum = _dot(tri, l_hi) + _dot(tri, l_mid)
    fill(1)
    cum_end = jnp.concatenate(
        [jnp.broadcast_to(cum[(c + 1) * t - 1:(c + 1) * t, :], (t, w))
         for c in range(tb // t)], axis=0)
    e_neg = jnp.exp(-cum)
    a_f = -kk * jnp.exp(cum - logw)
    r_f = r * jnp.exp(cum)
    a_t = a_f.astype(BF16)
    r_t = r_f.astype(BF16)
    b_h = bb * e_neg
    k_h = k2 * e_neg
    w_end = jnp.exp(cum_end)
    to_end = w_end * e_neg
    b_w = bb * to_end
    k_w = k2 * to_end
    fill(1)

    ri = lax.broadcasted_iota(jnp.int32, (t, pw), 0)
    ci = lax.broadcasted_iota(jnp.int32, (t, pw), 1)
    cm = jnp.bitwise_and(ci, t - 1)
    incl = ri >= cm
    strict = ri > cm
    eye = jnp.where(ri == cm, 1.0, 0.0)
    lane0 = ci < n
    si = lax.broadcasted_iota(jnp.int32, (pw, pw), 0)
    sj = lax.broadcasted_iota(jnp.int32, (pw, pw), 1)
    same_head = (si < n) == (sj < n)

    def bd(x):
        return jnp.concatenate(
            [jnp.where(lane0, x, 0.0), jnp.where(lane0, 0.0, x)], axis=0).astype(BF16)

    nq = RWKV_HEADS // 2
    tiles = [(c, q) for c in range(tb // t) for q in range(nq)]

    def tile(arr, c, q):
        return arr[c * t:(c + 1) * t, q * pw:(q + 1) * pw]

    sc = {}
    for cq in tiles:
        ar = jnp.concatenate([tile(a_t, *cq), tile(r_t, *cq)], axis=0)
        bk = jnp.concatenate([bd(tile(b_h, *cq)), bd(tile(k_h, *cq))], axis=0)
        sc[cq] = _dot_nt(ar, bk)
    lm, m_rb, l_ab, x = {}, {}, {}, {}
    for cq in tiles:
        l_ab[cq] = jnp.where(strict, sc[cq][0:t, 0:pw], 0.0)
        lm[cq] = jnp.concatenate(
            [jnp.where(strict, sc[cq][0:t, pw:], 0.0),
             jnp.where(incl, sc[cq][t:, pw:], 0.0)], axis=0).astype(BF16)
        m_rb[cq] = jnp.where(incl, sc[cq][t:, 0:pw], 0.0).astype(BF16)
    size = 2
    same_block = (ri // size) == (cm // size)
    for cq in tiles:
        x[cq] = eye + jnp.where(same_block, l_ab[cq], 0.0)
    while size < t:
        join = jnp.logical_and((ri // (2 * size)) == (cm // (2 * size)),
                               (ri // size) != (cm // size))
        join0 = jnp.logical_and(join, lane0)
        join1 = jnp.logical_and(join, jnp.logical_not(lane0))
        for cq in tiles:
            c_bd = jnp.concatenate([jnp.where(join0, l_ab[cq], 0.0),
                                    jnp.where(join1, l_ab[cq], 0.0)], axis=0).astype(BF16)
            xc = _dot(x[cq].astype(BF16), c_bd)
            x[cq] = x[cq] + _dot(xc.astype(BF16), bd(x[cq]))
        size *= 2
    for cq in tiles:
        x[cq] = x[cq].astype(BF16)

    lvov, xac = {}, {}
    for cq in tiles:
        lvov[cq] = _dot(lm[cq], bd(tile(v, *cq)))
    for cq in tiles:
        rhs = jnp.concatenate([bd(tile(a_f, *cq)), bd(lvov[cq][0:t])], axis=1)
        xac[cq] = _dot(x[cq], rhs)
    mq, nn, oc, wcb = {}, {}, {}, {}
    zero_tile = jnp.zeros((t, pw), F32)
    for cq in tiles:
        bkt = jnp.transpose(
            jnp.concatenate([tile(b_w, *cq), tile(k_w, *cq)], axis=0)).astype(BF16)
        low = jnp.concatenate([zero_tile, tile(v, *cq)], axis=1)
        mn = _dot(bkt, jnp.concatenate([xac[cq], low], axis=0).astype(BF16))
        qo = _dot(m_rb[cq], jnp.concatenate(
            [bd(xac[cq][:, 0:pw]), bd(xac[cq][:, pw:])], axis=1))
        mq[cq] = jnp.concatenate(
            [jnp.where(same_head, mn[:, 0:pw], 0.0),
             tile(r_f, *cq) + qo[:, 0:pw]], axis=0).astype(BF16)
        nn[cq] = jnp.where(same_head, mn[:, pw:], 0.0)
        oc[cq] = qo[:, pw:] + lvov[cq][t:]
        c, q = cq
        wcb[cq] = jnp.transpose(jnp.broadcast_to(
            w_end[c * t:c * t + 1, q * pw:(q + 1) * pw], (pw, pw)))

    fill(1)
    h_cur = [state_ref[q] for q in range(nq)]
    for c in range(tb // t):
        for q in range(nq):
            mh = _dot(mq[c, q], h_cur[q].astype(BF16))
            y_ref[c * t:(c + 1) * t, q * pw:(q + 1) * pw] = mh[pw:] + oc[c, q]
            h_cur[q] = wcb[c, q] * h_cur[q] + mh[0:pw] + nn[c, q]
        fill(1)
    for q in range(nq):
        state_ref[q] = h_cur[q]

    fill(2)
    y = y_ref[...]
    d = y - gsum([y])[0] * (1.0 / n)
    var = gsum([d * d])[0] * (1.0 / n)
    yn = d * lax.rsqrt(var + RWKV_GN_EPS) * lnw_ref[...] + lnb_ref[...]
    return (yn + bonus) * g


def _mixer_kernel(blocks_per_row,
                  x_ref, nw_ref, win_ref, cos_ref, sin_ref, mask_ref, kte_ref, qfs_ref,
                  cd_ref, rnw_ref, ones_ref, mu_ref, w0_ref, a0_ref, wwa_ref, g2_ref,
                  kk_ref, ka_ref, rk_ref, lnw_ref, lnb_ref, gsum_ref, tri_ref,
                  o_ref, proj_a_ref, proj_b_ref, ret_state_ref, rw_state_ref, carry_ref,
                  y_ref):
    i = pl.program_id(0)
    d = RET_HEAD_DIM

    @pl.when(i == 0)
    def _():
        proj_b_ref[...] = jnp.zeros_like(proj_b_ref)

    @pl.when(jnp.logical_or(i == 0, (i + blocks_per_row - 1) % blocks_per_row == 0))
    def _():
        ret_state_ref[...] = jnp.zeros_like(ret_state_ref)
        rw_state_ref[...] = jnp.zeros_like(rw_state_ref)
        carry_ref[...] = jnp.zeros_like(carry_ref)

    def step(new_ref, mix_ref):
        hn = _rms(x_ref[...], nw_ref[...]).astype(BF16)
        cols = iter(range(0, IN_COLS, PROJ_CHUNK))

        def fill(count):
            for _ in range(count):
                lo = next(cols, None)
                if lo is not None:
                    new_ref[:, lo:lo + PROJ_CHUNK] = _dot(hn, win_ref[:, lo:lo + PROJ_CHUNK])

        y_rw = _rwkv_block(mix_ref[:, RET_COLS:], mu_ref, w0_ref, a0_ref,
                           wwa_ref, g2_ref, kk_ref, ka_ref, rk_ref, lnw_ref, lnb_ref,
                           gsum_ref, tri_ref, rw_state_ref, carry_ref, y_ref, fill)
        o_ref[:, RET_WIDTH:] = y_rw.astype(BF16)

        tiles, y_ret = _retention_block(
            lambda lo, hi: mix_ref[:, lo:hi], cos_ref[...], sin_ref[...],
            mask_ref, kte_ref, qfs_ref, cd_ref, rnw_ref, ones_ref[...], ret_state_ref, fill)
        for (ci, h), yt in zip(tiles, y_ret):
            o_ref[ci * RET_CHUNK:(ci + 1) * RET_CHUNK, h * d:(h + 1) * d] = yt.astype(BF16)
        fill(IN_COLS // PROJ_CHUNK)

    pl.when(i % 2 == 0)(functools.partial(step, proj_a_ref, proj_b_ref))
    pl.when(i % 2 == 1)(functools.partial(step, proj_b_ref, proj_a_ref))


def _mixer(x2, seq, nw, w_in_b, rnw, mu, w0, a0, wwa, g2, k_k, k_a, r_k, ln_w, ln_b):
    m = x2.shape[0]
    tb = MIX_BLOCK
    nblk = m // tb
    bpr = seq // tb
    c, d, hh = RET_CHUNK, RET_HEAD_DIM, RET_HEADS
    t, n, w = RWKV_CHUNK, RWKV_HEAD_DIM, RWKV_WIDTH
    cos, sin, mask, kte, qfs, cd = _retention_tables(seq)
    ones = jnp.full((d, d), 1.0 / d, BF16)
    grp = jnp.arange(w // 2) // n
    gmat = (grp[:, None] == grp[None, :]).astype(BF16)
    ti = jnp.arange(tb)
    tri = ((ti[:, None] >= ti[None, :])
           & (ti[:, None] // t == ti[None, :] // t)).astype(BF16)
    vec = _const_spec((1, w))
    tab_spec = pl.BlockSpec((tb, d), lambda i: ((i + bpr - 1) % bpr, 0))
    return pl.pallas_call(
        functools.partial(_mixer_kernel, bpr),
        grid=(nblk + 1,),
        in_specs=[
            pl.BlockSpec((tb, D_MODEL), lambda i: (jnp.minimum(i, nblk - 1), 0)),
            _const_spec((1, D_MODEL)),
            _const_spec((D_MODEL, IN_COLS)),
            tab_spec, tab_spec,
            _const_spec((hh, c, c)),
            _const_spec((hh, c, d)),
            _const_spec((hh, c, d)),
            _const_spec((hh, 1, d)),
            _const_spec((1, RET_WIDTH)),
            _const_spec((d, d)),
            _const_spec((1, RWKV_COLS)),
            vec, vec,
            _const_spec((DECAY_LORA + AAA_LORA, 2 * w)),
            _const_spec((GATE_LORA, w)),
            vec, vec, vec, vec, vec,
            _const_spec((w // 2, w // 2)),
            _const_spec((tb, tb)),
        ],
        out_specs=pl.BlockSpec((tb, D_MODEL), lambda i: (jnp.maximum(i - 1, 0), 0)),
        out_shape=jax.ShapeDtypeStruct((m, D_MODEL), BF16),
        scratch_shapes=[
            pltpu.VMEM((tb, IN_COLS), F32),
            pltpu.VMEM((tb, IN_COLS), F32),
            pltpu.VMEM((hh, d, d), F32),
            pltpu.VMEM((RWKV_HEADS // 2, 2 * n, 2 * n), F32),
            pltpu.VMEM((1, RWKV_COLS), F32),
            pltpu.VMEM((tb, w), F32),
        ],
        compiler_params=pltpu.CompilerParams(
            dimension_semantics=("arbitrary",), vmem_limit_bytes=V7X_VMEM_LIMIT),
        name="mixer",
    )(x2, nw, w_in_b, cos, sin, mask, kte, qfs, cd, rnw, ones, mu, w0, a0, wwa, g2,
      k_k, k_a, r_k, ln_w, ln_b, gmat, tri)


def _tail_kernel(x_ref, mixed_ref, p_ref, wo_ref, nfw_ref, wg_ref, wu_ref,
                 wd_ref, npw_ref, wpg_ref, wpu_ref, fnw_ref, o_ref):
    rows = [slice(g * TAIL_GROUP, (g + 1) * TAIL_GROUP) for g in range(TOKEN_TILE // TAIL_GROUP)]
    xs = [x_ref[r] + _dot(mixed_ref[r], wo_ref[...]) for r in rows]
    hfs = [_rms(x, nfw_ref[...]).astype(BF16) for x in xs]
    acts = []
    for hf in hfs:
        gate = _dot(hf, wg_ref[...])
        up = _dot(hf, wu_ref[...])
        acts.append((gate * _sigmoid(gate) * up).astype(BF16))
    xs = [x + _dot(act, wd_ref[...]) for x, act in zip(xs, acts)]
    hps = [_rms(x, npw_ref[...]).astype(BF16) for x in xs]
    pgs = [_sigmoid(_dot(hp, wpg_ref[...])) for hp in hps]
    for r, x, pg in zip(rows, xs, pgs):
        x = x + _dot(p_ref[r].astype(BF16), wpu_ref[...]) * pg
        o_ref[r] = _rms(x, fnw_ref[...])


def _tail(x2, mixed, p2, w_o, nfw, w_gate, w_up, w_down, npw, w_pg, w_pu, fnw):
    m = x2.shape[0]
    tm = TOKEN_TILE
    vec = _const_spec((1, D_MODEL))
    return pl.pallas_call(
        _tail_kernel,
        grid=(m // tm,),
        in_specs=[
            pl.BlockSpec((tm, D_MODEL), lambda i: (i, 0)),
            pl.BlockSpec((tm, D_MODEL), lambda i: (i, 0)),
            pl.BlockSpec((tm, D_PLE), lambda i: (i, 0)),
            _const_spec((D_MODEL, D_MODEL)),
            vec,
            _const_spec((D_MODEL, D_FF)),
            _const_spec((D_MODEL, D_FF)),
            _const_spec((D_FF, D_MODEL)),
            vec,
            _const_spec((D_MODEL, D_MODEL)),
            _const_spec((D_PLE, D_MODEL)),
            vec,
        ],
        out_specs=pl.BlockSpec((tm, D_MODEL), lambda i: (i, 0)),
        out_shape=jax.ShapeDtypeStruct((m, D_MODEL), F32),
        compiler_params=pltpu.CompilerParams(
            dimension_semantics=("arbitrary",), vmem_limit_bytes=V7X_VMEM_LIMIT),
        name="tail",
    )(x2, mixed, p2, w_o, nfw, w_gate, w_up, w_down, npw, w_pg, w_pu, fnw)


def kernel(x, p, norm_mix_w, w_in, ret_norm_w, rw_mu, rw_w0, rw_w2, rw_a0, rw_a2,
           rw_g2, rw_k_k, rw_k_a, rw_r_k, rw_ln_w, rw_ln_b, w_o, norm_ffn_w,
           w_gate, w_up, w_down, norm_ple_w, w_ple_gate, w_ple_up, final_norm_w):
    b, s, dm = x.shape
    assert w_in.shape[0] == 1, "single-layer trunk only"
    assert s % MIX_BLOCK == 0 and (b * s) % TOKEN_TILE == 0
    x2 = x.reshape(b * s, dm)
    row = lambda a: a.reshape(1, -1)
    w_in_b = w_in[0].astype(BF16)
    w_in_b = jnp.concatenate([_even_odd_columns(w_in_b[:, :2 * RET_WIDTH]),
                              w_in_b[:, 2 * RET_WIDTH:]], axis=1)
    zeros = jnp.zeros((DECAY_LORA, RWKV_WIDTH), F32)
    wwa = jnp.concatenate([
        jnp.concatenate([rw_w2[0], zeros], axis=1),
        jnp.concatenate([zeros, rw_a2[0]], axis=1)], axis=0).astype(BF16)
    mixed = _mixer(x2, s, row(norm_mix_w[0]), w_in_b, row(ret_norm_w[0]), row(rw_mu[0]),
                   row(rw_w0[0]), row(rw_a0[0]), wwa, rw_g2[0].astype(BF16),
                   row(rw_k_k[0]), row(rw_k_a[0]), row(rw_r_k[0]), row(rw_ln_w[0]),
                   row(rw_ln_b[0]))
    out = _tail(x2, mixed, p[0].reshape(b * s, D_PLE), w_o[0].astype(BF16),
                row(norm_ffn_w[0]), w_gate[0].astype(BF16), w_up[0].astype(BF16),
                w_down[0].astype(BF16), row(norm_ple_w[0]), w_ple_gate[0].astype(BF16),
                w_ple_up[0].astype(BF16), row(final_norm_w))
    return out.reshape(b, s, dm)
```

```python
import functools

import jax
import jax.numpy as jnp
import numpy as np
from jax import lax
from jax.experimental import pallas as pl
from jax.experimental.pallas import tpu as pltpu

F32 = jnp.float32
BF16 = jnp.bfloat16

D_MODEL = 1024
D_PLE = 256
RET_HEAD_DIM = 128
RET_WIDTH = 512
RET_HEADS = 4
RET_CHUNK = 128
RET_COLS = 4 * RET_WIDTH
RWKV_HEAD_DIM = 64
RWKV_WIDTH = 512
RWKV_HEADS = 8
DECAY_LORA = 64
AAA_LORA = 64
GATE_LORA = 128
RWKV_COLS = 3 * RWKV_WIDTH + DECAY_LORA + AAA_LORA + GATE_LORA
IN_COLS = RET_COLS + RWKV_COLS
D_FF = 2816
NORM_EPS = 1e-6
RET_GN_EPS = 1e-5
RWKV_GN_EPS = 64e-5

RWKV_CHUNK = 64
INV_BASE = 8
MIX_BLOCK = 256
PROJ_CHUNK = 256
TOKEN_TILE = 512
TAIL_GROUP = 256
V7X_VMEM_LIMIT = 56 * 1024 * 1024


def _dot(a, b):
    return jnp.dot(a, b, preferred_element_type=F32)


def _dot_nt(a, b):
    return lax.dot_general(a, b, (((1,), (1,)), ((), ())), preferred_element_type=F32)


def _dot_tn(a, b):
    return lax.dot_general(a, b, (((0,), (0,)), ((), ())), preferred_element_type=F32)


def _rms(x, w):
    ms = jnp.mean(x * x, axis=-1, keepdims=True)
    return x * lax.rsqrt(ms + NORM_EPS) * w


def _sigmoid(x):
    return 0.5 * jnp.tanh(0.5 * x) + 0.5


def _const_spec(shape):
    nd = len(shape)
    return pl.BlockSpec(shape, lambda *_: (0,) * nd, pipeline_mode=pl.Buffered(1))


def _retention_block(load, cos, sin, mask_ref, kte_ref, qfs_ref, cd_ref, nw_ref, ones,
                     state_ref, fill):
    c, d = RET_CHUNK, RET_HEAD_DIM
    nchunk = MIX_BLOCK // c

    def rot(t):
        return t * cos + pltpu.roll(t, d // 2, 1) * sin

    tiles = [(ci, h) for ci in range(nchunk) for h in range(RET_HEADS)]
    q, k, vb, gate = {}, {}, {}, {}
    for h in range(RET_HEADS):
        lo = h * d
        qh = rot(load(lo, lo + d))
        kh = rot(load(RET_WIDTH + lo, RET_WIDTH + lo + d))
        vh = load(2 * RET_WIDTH + lo, 2 * RET_WIDTH + lo + d).astype(BF16)
        gh = load(3 * RET_WIDTH + lo, 3 * RET_WIDTH + lo + d)
        gh = gh * _sigmoid(gh) * nw_ref[:, lo:lo + d]
        for ci in range(nchunk):
            rows = slice(ci * c, (ci + 1) * c)
            q[ci, h], k[ci, h], vb[ci, h], gate[ci, h] = qh[rows], kh[rows], vh[rows], gh[rows]
    fill(2)
    lhs, kd = {}, {}
    for cq in tiles:
        h = cq[1]
        scores = _dot_nt(q[cq].astype(BF16), k[cq].astype(BF16)) * mask_ref[h]
        lhs[cq] = jnp.concatenate([scores, q[cq] * qfs_ref[h]], axis=1).astype(BF16)
        kd[cq] = (k[cq] * kte_ref[h]).astype(BF16)
    r_cur = [state_ref[h] for h in range(RET_HEADS)]
    ys = []
    for ci in range(nchunk):
        for h in range(RET_HEADS):
            rhs = jnp.concatenate([vb[ci, h], r_cur[h].astype(BF16)], axis=0)
            ys.append(_dot(lhs[ci, h], rhs))
            r_cur[h] = r_cur[h] * cd_ref[h] + _dot_tn(kd[ci, h], vb[ci, h])
    for h in range(RET_HEADS):
        state_ref[h] = r_cur[h]

    fill(1)
    y = jnp.concatenate(ys, axis=0)
    dlt = y - _dot(y.astype(BF16), ones)
    var = _dot((dlt * dlt).astype(BF16), ones)
    yn = dlt * lax.rsqrt(var + RET_GN_EPS)
    return tiles, [yn[i * c:(i + 1) * c] * gate[cq] for i, cq in enumerate(tiles)]


def _retention_tables(seq):
    d, c, hh = RET_HEAD_DIM, RET_CHUNK, RET_HEADS
    pos = jnp.arange(seq, dtype=F32)
    angle = 1.0 / (10000.0 ** jnp.linspace(0.0, 1.0, d // 2, dtype=F32))
    phase = pos[:, None] * angle[None, :]
    cos = jnp.concatenate([jnp.cos(phase), jnp.cos(phase)], axis=1)
    sin = jnp.concatenate([-jnp.sin(phase), jnp.sin(phase)], axis=1)
    log_g = jnp.log(1.0 - 2.0 ** (-5.0 - jnp.arange(hh, dtype=F32)))
    idx = jnp.arange(c, dtype=F32)
    rel = idx[:, None] - idx[None, :]
    scale = d ** -0.5
    mask = jnp.where(rel[None] >= 0,
                     jnp.exp(log_g[:, None, None] * jnp.maximum(rel, 0.0)[None]), 0.0) * scale
    kte = jnp.exp(log_g[:, None] * (c - 1 - idx)[None]) * scale
    kte = jnp.broadcast_to(kte[:, :, None], (hh, c, d))
    qfs = jnp.exp(log_g[:, None] * (idx + 1.0)[None])
    qfs = jnp.broadcast_to(qfs[:, :, None], (hh, c, d))
    cd = jnp.broadcast_to(jnp.exp(log_g * c)[:, None, None], (hh, 1, d))
    return cos, sin, mask, kte, qfs, cd


def _even_odd_columns(w_cols):
    rows = w_cols.shape[0]
    w4 = w_cols.reshape(rows, -1, RET_HEAD_DIM // 2, 2)
    return jnp.swapaxes(w4, 2, 3).reshape(rows, -1)


def _split2(x):
    hi = x.astype(BF16)
    mid = (x - hi.astype(F32)).astype(BF16)
    return hi, mid


def _rwkv_block(h, mu_ref, w0_ref, a0_ref, wwa_ref, g2_ref, kk_ref, ka_ref, rk_ref,
                lnw_ref, lnb_ref, gsum_ref, tri_ref, state_ref, carry_ref, y_ref, fill):
    t = RWKV_CHUNK
    n = RWKV_HEAD_DIM
    w = RWKV_WIDTH
    tb = MIX_BLOCK
    pw = 2 * n

    row = lax.broadcasted_iota(jnp.int32, h.shape, 0)
    prev = jnp.where(row == 0, carry_ref[...], pltpu.roll(h, 1, 0))
    carry_ref[...] = h[tb - 1:tb, :]
    hs = h + (prev - h) * mu_ref[...]
    fill(1)

    r = hs[:, 0:w]
    k = hs[:, w:2 * w]
    v = hs[:, 2 * w:3 * w]
    wa = hs[:, 3 * w:3 * w + DECAY_LORA + AAA_LORA]
    gd = hs[:, 3 * w + DECAY_LORA + AAA_LORA:]
    lane = lax.broadcasted_iota(jnp.int32, wa.shape, 1)
    wa = jnp.where(lane < DECAY_LORA, jnp.tanh(wa), wa)
    lora = _dot(wa.astype(BF16), wwa_ref[...])
    wp = w0_ref[...] + lora[:, :w]
    logw = (-float(np.exp(-0.5))) * _sigmoid(wp)
    a = _sigmoid(a0_ref[...] + lora[:, w:])
    g = _dot(_sigmoid(gd).astype(BF16), g2_ref[...])
    fill(1)

    gmat = gsum_ref[...]
    half = w // 2

    def gsum(xs):
        stack = jnp.concatenate(
            [x[:, i * half:(i + 1) * half] for x in xs for i in range(2)], axis=0)
        res = _dot(stack.astype(BF16), gmat)
        return [jnp.concatenate([res[2 * j * tb:(2 * j + 1) * tb],
                                 res[(2 * j + 1) * tb:(2 * j + 2) * tb]], axis=1)
                for j in range(len(xs))]

    kk = k * kk_ref[...]
    k2 = k * (1.0 + (a - 1.0) * ka_ref[...])
    kk_sq, rk_sum = gsum([kk * kk, r * k2 * rk_ref[...]])
    kk = kk * lax.rsqrt(jnp.maximum(kk_sq, 1e-24))
    bonus = rk_sum * v
    bb = kk * a
    fill(1)

    tri = tri_ref[...]
    l_hi, l_mid = _split2(logw)
    cum = _dot(tri, l_hi) + _dot(tri, l_mid)
    fill(1)
    cum_end = jnp.concatenate(
        [jnp.broadcast_to(cum[(c + 1) * t - 1:(c + 1) * t, :], (t, w))
         for c in range(tb // t)], axis=0)
    e_neg = jnp.exp(-cum)
    a_f = -kk * jnp.exp(cum - logw)
    r_f = r * jnp.exp(cum)
    a_t = a_f.astype(BF16)
    r_t = r_f.astype(BF16)
    b_h = bb * e_neg
    k_h = k2 * e_neg
    w_end = jnp.exp(cum_end)
    to_end = w_end * e_neg
    b_w = bb * to_end
    k_w = k2 * to_end
    fill(1)

    ri = lax.broadcasted_iota(jnp.int32, (t, pw), 0)
    ci = lax.broadcasted_iota(jnp.int32, (t, pw), 1)
    cm = jnp.bitwise_and(ci, t - 1)
    incl = ri >= cm
    strict = ri > cm
    eye = jnp.where(ri == cm, 1.0, 0.0)
    lane0 = ci < n
    si = lax.broadcasted_iota(jnp.int32, (pw, pw), 0)
    sj = lax.broadcasted_iota(jnp.int32, (pw, pw), 1)
    same_head = (si < n) == (sj < n)

    def bd(x):
        xb = x.astype(BF16)
        zb = jnp.zeros_like(xb)
        return jnp.concatenate([jnp.where(lane0, xb, zb), jnp.where(lane0, zb, xb)], axis=0)

    nq = RWKV_HEADS // 2
    tiles = [(c, q) for c in range(tb // t) for q in range(nq)]

    def tile(arr, c, q):
        return arr[c * t:(c + 1) * t, q * pw:(q + 1) * pw]

    sc = {}
    for cq in tiles:
        ar = jnp.concatenate([tile(a_t, *cq), tile(r_t, *cq)], axis=0)
        bk = jnp.concatenate([bd(tile(b_h, *cq)), bd(tile(k_h, *cq))], axis=0)
        sc[cq] = _dot_nt(ar, bk)
    lm, m_rb, l_ab, x = {}, {}, {}, {}
    for cq in tiles:
        l_ab[cq] = jnp.where(strict, sc[cq][0:t, 0:pw], 0.0)
        lm[cq] = jnp.concatenate(
            [jnp.where(strict, sc[cq][0:t, pw:], 0.0),
             jnp.where(incl, sc[cq][t:, pw:], 0.0)], axis=0).astype(BF16)
        m_rb[cq] = jnp.where(incl, sc[cq][t:, 0:pw], 0.0).astype(BF16)
    size = INV_BASE
    same_block = (ri // size) == (cm // size)
    powers = {}
    for cq in tiles:
        l_blk = jnp.where(same_block, l_ab[cq], 0.0)
        x[cq] = eye + l_blk
        powers[cq] = _dot(l_blk.astype(BF16), bd(l_blk))
    for _ in range(int(np.log2(size)) - 2):
        for cq in tiles:
            z = _dot(jnp.concatenate([powers[cq], x[cq]], axis=0).astype(BF16),
                     bd(powers[cq]))
            powers[cq] = z[0:t]
            x[cq] = x[cq] + z[t:]
    for cq in tiles:
        x[cq] = x[cq] + _dot(x[cq].astype(BF16), bd(powers[cq]))
    while size < t:
        join = jnp.logical_and((ri // (2 * size)) == (cm // (2 * size)),
                               (ri // size) != (cm // size))
        join0 = jnp.logical_and(join, lane0)
        join1 = jnp.logical_and(join, jnp.logical_not(lane0))
        xc = {}
        for cq in tiles:
            c_bd = jnp.concatenate([jnp.where(join0, l_ab[cq], 0.0),
                                    jnp.where(join1, l_ab[cq], 0.0)], axis=0).astype(BF16)
            xc[cq] = _dot(x[cq].astype(BF16), c_bd)
        for cq in tiles:
            x[cq] = x[cq] + _dot(xc[cq].astype(BF16), bd(x[cq]))
        size *= 2
    for cq in tiles:
        x[cq] = x[cq].astype(BF16)

    lvov, xac = {}, {}
    for cq in tiles:
        lvov[cq] = _dot(lm[cq], bd(tile(v, *cq)))
    for cq in tiles:
        rhs = jnp.concatenate([bd(tile(a_f, *cq)), bd(lvov[cq][0:t])], axis=1)
        xac[cq] = _dot(x[cq], rhs)
    mq, nn, oc, wcb = {}, {}, {}, {}
    zero_tile = jnp.zeros((t, pw), F32)
    for cq in tiles:
        bkt = jnp.transpose(
            jnp.concatenate([tile(b_w, *cq), tile(k_w, *cq)], axis=0)).astype(BF16)
        low = jnp.concatenate([zero_tile, tile(v, *cq)], axis=1)
        mn = _dot(bkt, jnp.concatenate([xac[cq], low], axis=0).astype(BF16))
        qo = _dot(m_rb[cq], jnp.concatenate(
            [bd(xac[cq][:, 0:pw]), bd(xac[cq][:, pw:])], axis=1))
        mq[cq] = jnp.concatenate(
            [jnp.where(same_head, mn[:, 0:pw], 0.0),
             tile(r_f, *cq) + qo[:, 0:pw]], axis=0).astype(BF16)
        nn[cq] = jnp.where(same_head, mn[:, pw:], 0.0)
        oc[cq] = qo[:, pw:] + lvov[cq][t:]
        c, q = cq
        wcb[cq] = jnp.transpose(jnp.broadcast_to(
            w_end[c * t:c * t + 1, q * pw:(q + 1) * pw], (pw, pw)))

    fill(1)
    h_cur = [state_ref[q] for q in range(nq)]
    for c in range(tb // t):
        for q in range(nq):
            mh = _dot(mq[c, q], h_cur[q].astype(BF16))
            y_ref[c * t:(c + 1) * t, q * pw:(q + 1) * pw] = mh[pw:] + oc[c, q]
            h_cur[q] = wcb[c, q] * h_cur[q] + mh[0:pw] + nn[c, q]
        fill(1)
    for q in range(nq):
        state_ref[q] = h_cur[q]

    fill(2)
    y = y_ref[...]
    d = y - gsum([y])[0] * (1.0 / n)
    var = gsum([d * d])[0] * (1.0 / n)
    yn = d * lax.rsqrt(var + RWKV_GN_EPS) * lnw_ref[...] + lnb_ref[...]
    return (yn + bonus) * g


def _mixer_kernel(blocks_per_row,
                  x_ref, nw_ref, win_ref, cos_ref, sin_ref, mask_ref, kte_ref, qfs_ref,
                  cd_ref, rnw_ref, ones_ref, mu_ref, w0_ref, a0_ref, wwa_ref, g2_ref,
                  kk_ref, ka_ref, rk_ref, lnw_ref, lnb_ref, gsum_ref, tri_ref,
                  o_ref, proj_a_ref, proj_b_ref, ret_state_ref, rw_state_ref, carry_ref,
                  y_ref):
    i = pl.program_id(0)
    d = RET_HEAD_DIM

    @pl.when(i == 0)
    def _():
        proj_b_ref[...] = jnp.zeros_like(proj_b_ref)

    @pl.when(jnp.logical_or(i == 0, (i + blocks_per_row - 1) % blocks_per_row == 0))
    def _():
        ret_state_ref[...] = jnp.zeros_like(ret_state_ref)
        rw_state_ref[...] = jnp.zeros_like(rw_state_ref)
        carry_ref[...] = jnp.zeros_like(carry_ref)

    def step(new_ref, mix_ref):
        hn = _rms(x_ref[...], nw_ref[...]).astype(BF16)
        cols = iter(range(0, IN_COLS, PROJ_CHUNK))

        def fill(count):
            for _ in range(count):
                lo = next(cols, None)
                if lo is not None:
                    new_ref[:, lo:lo + PROJ_CHUNK] = _dot(hn, win_ref[:, lo:lo + PROJ_CHUNK])

        y_rw = _rwkv_block(mix_ref[:, RET_COLS:], mu_ref, w0_ref, a0_ref,
                           wwa_ref, g2_ref, kk_ref, ka_ref, rk_ref, lnw_ref, lnb_ref,
                           gsum_ref, tri_ref, rw_state_ref, carry_ref, y_ref, fill)
        o_ref[:, RET_WIDTH:] = y_rw.astype(BF16)

        tiles, y_ret = _retention_block(
            lambda lo, hi: mix_ref[:, lo:hi], cos_ref[...], sin_ref[...],
            mask_ref, kte_ref, qfs_ref, cd_ref, rnw_ref, ones_ref[...], ret_state_ref, fill)
        for (ci, h), yt in zip(tiles, y_ret):
            o_ref[ci * RET_CHUNK:(ci + 1) * RET_CHUNK, h * d:(h + 1) * d] = yt.astype(BF16)
        fill(IN_COLS // PROJ_CHUNK)

    pl.when(i % 2 == 0)(functools.partial(step, proj_a_ref, proj_b_ref))
    pl.when(i % 2 == 1)(functools.partial(step, proj_b_ref, proj_a_ref))


def _mixer(x2, seq, nw, w_in_b, rnw, mu, w0, a0, wwa, g2, k_k, k_a, r_k, ln_w, ln_b):
    m = x2.shape[0]
    tb = MIX_BLOCK
    nblk = m // tb
    bpr = seq // tb
    c, d, hh = RET_CHUNK, RET_HEAD_DIM, RET_HEADS
    t, n, w = RWKV_CHUNK, RWKV_HEAD_DIM, RWKV_WIDTH
    cos, sin, mask, kte, qfs, cd = _retention_tables(seq)
    ones = jnp.full((d, d), 1.0 / d, BF16)
    grp = jnp.arange(w // 2) // n
    gmat = (grp[:, None] == grp[None, :]).astype(BF16)
    ti = jnp.arange(tb)
    tri = ((ti[:, None] >= ti[None, :])
           & (ti[:, None] // t == ti[None, :] // t)).astype(BF16)
    vec = _const_spec((1, w))
    tab_spec = pl.BlockSpec((tb, d), lambda i: ((i + bpr - 1) % bpr, 0))
    return pl.pallas_call(
        functools.partial(_mixer_kernel, bpr),
        grid=(nblk + 1,),
        in_specs=[
            pl.BlockSpec((tb, D_MODEL), lambda i: (jnp.minimum(i, nblk - 1), 0)),
            _const_spec((1, D_MODEL)),
            _const_spec((D_MODEL, IN_COLS)),
            tab_spec, tab_spec,
            _const_spec((hh, c, c)),
            _const_spec((hh, c, d)),
            _const_spec((hh, c, d)),
            _const_spec((hh, 1, d)),
            _const_spec((1, RET_WIDTH)),
            _const_spec((d, d)),
            _const_spec((1, RWKV_COLS)),
            vec, vec,
            _const_spec((DECAY_LORA + AAA_LORA, 2 * w)),
            _const_spec((GATE_LORA, w)),
            vec, vec, vec, vec, vec,
            _const_spec((w // 2, w // 2)),
            _const_spec((tb, tb)),
        ],
        out_specs=pl.BlockSpec((tb, D_MODEL), lambda i: (jnp.maximum(i - 1, 0), 0)),
        out_shape=jax.ShapeDtypeStruct((m, D_MODEL), BF16),
        scratch_shapes=[
            pltpu.VMEM((tb, IN_COLS), F32),
            pltpu.VMEM((tb, IN_COLS), F32),
            pltpu.VMEM((hh, d, d), F32),
            pltpu.VMEM((RWKV_HEADS // 2, 2 * n, 2 * n), F32),
            pltpu.VMEM((1, RWKV_COLS), F32),
            pltpu.VMEM((tb, w), F32),
        ],
        compiler_params=pltpu.CompilerParams(
            dimension_semantics=("arbitrary",), vmem_limit_bytes=V7X_VMEM_LIMIT),
        name="mixer",
    )(x2, nw, w_in_b, cos, sin, mask, kte, qfs, cd, rnw, ones, mu, w0, a0, wwa, g2,
      k_k, k_a, r_k, ln_w, ln_b, gmat, tri)


def _tail_kernel(x_ref, mixed_ref, p_ref, wo_ref, nfw_ref, wg_ref, wu_ref,
                 wd_ref, npw_ref, wpg_ref, wpu_ref, fnw_ref, o_ref):
    rows = [slice(g * TAIL_GROUP, (g + 1) * TAIL_GROUP) for g in range(TOKEN_TILE // TAIL_GROUP)]
    xs = [x_ref[r] + _dot(mixed_ref[r], wo_ref[...]) for r in rows]
    hfs = [_rms(x, nfw_ref[...]).astype(BF16) for x in xs]
    acts = []
    for hf in hfs:
        gate = _dot(hf, wg_ref[...])
        up = _dot(hf, wu_ref[...])
        acts.append((gate * _sigmoid(gate) * up).astype(BF16))
    xs = [x + _dot(act, wd_ref[...]) for x, act in zip(xs, acts)]
    hps = [_rms(x, npw_ref[...]).astype(BF16) for x in xs]
    pgs = [_sigmoid(_dot(hp, wpg_ref[...])) for hp in hps]
    for r, x, pg in zip(rows, xs, pgs):
        x = x + _dot(p_ref[r].astype(BF16), wpu_ref[...]) * pg
        o_ref[r] = _rms(x, fnw_ref[...])


def _tail(x2, mixed, p2, w_o, nfw, w_gate, w_up, w_down, npw, w_pg, w_pu, fnw):
    m = x2.shape[0]
    tm = TOKEN_TILE
    vec = _const_spec((1, D_MODEL))
    return pl.pallas_call(
        _tail_kernel,
        grid=(m // tm,),
        in_specs=[
            pl.BlockSpec((tm, D_MODEL), lambda i: (i, 0)),
            pl.BlockSpec((tm, D_MODEL), lambda i: (i, 0)),
            pl.BlockSpec((tm, D_PLE), lambda i: (i, 0)),
            _const_spec((D_MODEL, D_MODEL)),
            vec,
            _const_spec((D_MODEL, D_FF)),
            _const_spec((D_MODEL, D_FF)),
            _const_spec((D_FF, D_MODEL)),
            vec,
            _const_spec((D_MODEL, D_MODEL)),
            _const_spec((D_PLE, D_MODEL)),
            vec,
        ],
        out_specs=pl.BlockSpec((tm, D_MODEL), lambda i: (i, 0)),
        out_shape=jax.ShapeDtypeStruct((m, D_MODEL), F32),
        compiler_params=pltpu.CompilerParams(
            dimension_semantics=("arbitrary",), vmem_limit_bytes=V7X_VMEM_LIMIT),
        name="tail",
    )(x2, mixed, p2, w_o, nfw, w_gate, w_up, w_down, npw, w_pg, w_pu, fnw)


def kernel(x, p, norm_mix_w, w_in, ret_norm_w, rw_mu, rw_w0, rw_w2, rw_a0, rw_a2,
           rw_g2, rw_k_k, rw_k_a, rw_r_k, rw_ln_w, rw_ln_b, w_o, norm_ffn_w,
           w_gate, w_up, w_down, norm_ple_w, w_ple_gate, w_ple_up, final_norm_w):
    b, s, dm = x.shape
    assert w_in.shape[0] == 1, "single-layer trunk only"
    assert s % MIX_BLOCK == 0 and (b * s) % TOKEN_TILE == 0
    x2 = x.reshape(b * s, dm)
    row = lambda a: a.reshape(1, -1)
    w_in_b = w_in[0].astype(BF16)
    w_in_b = jnp.concatenate([_even_odd_columns(w_in_b[:, :2 * RET_WIDTH]),
                              w_in_b[:, 2 * RET_WIDTH:]], axis=1)
    zeros = jnp.zeros((DECAY_LORA, RWKV_WIDTH), F32)
    wwa = jnp.concatenate([
        jnp.concatenate([rw_w2[0], zeros], axis=1),
        jnp.concatenate([zeros, rw_a2[0]], axis=1)], axis=0).astype(BF16)
    mixed = _mixer(x2, s, row(norm_mix_w[0]), w_in_b, row(ret_norm_w[0]), row(rw_mu[0]),
                   row(rw_w0[0]), row(rw_a0[0]), wwa, rw_g2[0].astype(BF16),
                   row(rw_k_k[0]), row(rw_k_a[0]), row(rw_r_k[0]), row(rw_ln_w[0]),
                   row(rw_ln_b[0]))
    out = _tail(x2, mixed, p[0].reshape(b * s, D_PLE), w_o[0].astype(BF16),
                row(norm_ffn_w[0]), w_gate[0].astype(BF16), w_up[0].astype(BF16),
                w_down[0].astype(BF16), row(norm_ple_w[0]), w_ple_gate[0].astype(BF16),
                w_ple_up[0].astype(BF16), row(final_norm_w))
    return out.reshape(b, s, dm)
```

```python
import functools

import jax
import jax.numpy as jnp
import numpy as np
from jax import lax
from jax.experimental import pallas as pl
from jax.experimental.pallas import tpu as pltpu

F32 = jnp.float32
BF16 = jnp.bfloat16

D_MODEL = 1024
D_PLE = 256
RET_HEAD_DIM = 128
RET_WIDTH = 512
RET_HEADS = 4
RET_CHUNK = 128
RET_COLS = 4 * RET_WIDTH
RWKV_HEAD_DIM = 64
RWKV_WIDTH = 512
RWKV_HEADS = 8
DECAY_LORA = 64
AAA_LORA = 64
GATE_LORA = 128
RWKV_COLS = 3 * RWKV_WIDTH + DECAY_LORA + AAA_LORA + GATE_LORA
IN_COLS = RET_COLS + RWKV_COLS
D_FF = 2816
NORM_EPS = 1e-6
RET_GN_EPS = 1e-5
RWKV_GN_EPS = 64e-5

RWKV_CHUNK = 64
INV_BASE = 8
MIX_BLOCK = 256
PROJ_CHUNK = 256
TOKEN_TILE = 512
TAIL_GROUP = 256
FF_SPLIT = 1408
V7X_VMEM_LIMIT = 56 * 1024 * 1024


def _dot(a, b):
    return jnp.dot(a, b, preferred_element_type=F32)


def _dot_nt(a, b):
    return lax.dot_general(a, b, (((1,), (1,)), ((), ())), preferred_element_type=F32)


def _dot_tn(a, b):
    return lax.dot_general(a, b, (((0,), (0,)), ((), ())), preferred_element_type=F32)


def _rms(x, w):
    ms = jnp.mean(x * x, axis=-1, keepdims=True)
    return x * lax.rsqrt(ms + NORM_EPS) * w


def _sigmoid(x):
    return 0.5 * jnp.tanh(0.5 * x) + 0.5


def _const_spec(shape):
    nd = len(shape)
    return pl.BlockSpec(shape, lambda *_: (0,) * nd, pipeline_mode=pl.Buffered(1))


def _retention_block(load, cos, sin, mask_ref, kte_ref, qfs_ref, cd_ref, nw_ref, ones,
                     state_ref, fill):
    c, d = RET_CHUNK, RET_HEAD_DIM
    nchunk = MIX_BLOCK // c

    def rot(t):
        return t * cos + pltpu.roll(t, d // 2, 1) * sin

    tiles = [(ci, h) for ci in range(nchunk) for h in range(RET_HEADS)]
    q, k, vb, gate = {}, {}, {}, {}
    for h in range(RET_HEADS):
        lo = h * d
        qh = rot(load(lo, lo + d))
        kh = rot(load(RET_WIDTH + lo, RET_WIDTH + lo + d))
        vh = load(2 * RET_WIDTH + lo, 2 * RET_WIDTH + lo + d).astype(BF16)
        gh = load(3 * RET_WIDTH + lo, 3 * RET_WIDTH + lo + d)
        gh = gh * _sigmoid(gh) * nw_ref[:, lo:lo + d]
        for ci in range(nchunk):
            rows = slice(ci * c, (ci + 1) * c)
            q[ci, h], k[ci, h], vb[ci, h], gate[ci, h] = qh[rows], kh[rows], vh[rows], gh[rows]
    fill(2)
    lhs, kd = {}, {}
    for cq in tiles:
        h = cq[1]
        scores = _dot_nt(q[cq].astype(BF16), k[cq].astype(BF16)) * mask_ref[h]
        lhs[cq] = jnp.concatenate([scores, q[cq] * qfs_ref[h]], axis=1).astype(BF16)
        kd[cq] = (k[cq] * kte_ref[h]).astype(BF16)
    r_cur = [state_ref[h] for h in range(RET_HEADS)]
    ys = []
    for ci in range(nchunk):
        for h in range(RET_HEADS):
            rhs = jnp.concatenate([vb[ci, h], r_cur[h].astype(BF16)], axis=0)
            ys.append(_dot(lhs[ci, h], rhs))
            r_cur[h] = r_cur[h] * cd_ref[h] + _dot_tn(kd[ci, h], vb[ci, h])
    for h in range(RET_HEADS):
        state_ref[h] = r_cur[h]

    fill(1)
    y = jnp.concatenate(ys, axis=0)
    dlt = y - _dot(y.astype(BF16), ones)
    var = _dot((dlt * dlt).astype(BF16), ones)
    yn = dlt * lax.rsqrt(var + RET_GN_EPS)
    return tiles, [yn[i * c:(i + 1) * c] * gate[cq] for i, cq in enumerate(tiles)]


def _retention_tables(seq):
    d, c, hh = RET_HEAD_DIM, RET_CHUNK, RET_HEADS
    pos = jnp.arange(seq, dtype=F32)
    angle = 1.0 / (10000.0 ** jnp.linspace(0.0, 1.0, d // 2, dtype=F32))
    phase = pos[:, None] * angle[None, :]
    cos = jnp.concatenate([jnp.cos(phase), jnp.cos(phase)], axis=1)
    sin = jnp.concatenate([-jnp.sin(phase), jnp.sin(phase)], axis=1)
    log_g = jnp.log(1.0 - 2.0 ** (-5.0 - jnp.arange(hh, dtype=F32)))
    idx = jnp.arange(c, dtype=F32)
    rel = idx[:, None] - idx[None, :]
    scale = d ** -0.5
    mask = jnp.where(rel[None] >= 0,
                     jnp.exp(log_g[:, None, None] * jnp.maximum(rel, 0.0)[None]), 0.0) * scale
    kte = jnp.exp(log_g[:, None] * (c - 1 - idx)[None]) * scale
    kte = jnp.broadcast_to(kte[:, :, None], (hh, c, d))
    qfs = jnp.exp(log_g[:, None] * (idx + 1.0)[None])
    qfs = jnp.broadcast_to(qfs[:, :, None], (hh, c, d))
    cd = jnp.broadcast_to(jnp.exp(log_g * c)[:, None, None], (hh, 1, d))
    return cos, sin, mask, kte, qfs, cd


def _even_odd_columns(w_cols):
    rows = w_cols.shape[0]
    w4 = w_cols.reshape(rows, -1, RET_HEAD_DIM // 2, 2)
    return jnp.swapaxes(w4, 2, 3).reshape(rows, -1)


def _split2(x):
    hi = x.astype(BF16)
    mid = (x - hi.astype(F32)).astype(BF16)
    return hi, mid


def _rwkv_block(h, mu_ref, w0_ref, a0_ref, wwa_ref, g2_ref, kk_ref, ka_ref, rk_ref,
                lnw_ref, lnb_ref, gsum_ref, tri_ref, state_ref, carry_ref, y_ref, fill):
    t = RWKV_CHUNK
    n = RWKV_HEAD_DIM
    w = RWKV_WIDTH
    tb = MIX_BLOCK
    pw = 2 * n

    row = lax.broadcasted_iota(jnp.int32, h.shape, 0)
    prev = jnp.where(row == 0, carry_ref[...], pltpu.roll(h, 1, 0))
    carry_ref[...] = h[tb - 1:tb, :]
    hs = h + (prev - h) * mu_ref[...]
    fill(1)

    r = hs[:, 0:w]
    k = hs[:, w:2 * w]
    v = hs[:, 2 * w:3 * w]
    wa = hs[:, 3 * w:3 * w + DECAY_LORA + AAA_LORA]
    gd = hs[:, 3 * w + DECAY_LORA + AAA_LORA:]
    lane = lax.broadcasted_iota(jnp.int32, wa.shape, 1)
    wa = jnp.where(lane < DECAY_LORA, jnp.tanh(wa), wa)
    lora = _dot(wa.astype(BF16), wwa_ref[...])
    wp = w0_ref[...] + lora[:, :w]
    logw = (-float(np.exp(-0.5))) * _sigmoid(wp)
    a = _sigmoid(a0_ref[...] + lora[:, w:])
    g = _dot(_sigmoid(gd).astype(BF16), g2_ref[...])
    fill(1)

    gmat = gsum_ref[...]
    half = w // 2

    def gsum(xs):
        stack = jnp.concatenate(
            [x[:, i * half:(i + 1) * half] for x in xs for i in range(2)], axis=0)
        res = _dot(stack.astype(BF16), gmat)
        return [jnp.concatenate([res[2 * j * tb:(2 * j + 1) * tb],
                                 res[(2 * j + 1) * tb:(2 * j + 2) * tb]], axis=1)
                for j in range(len(xs))]

    kk = k * kk_ref[...]
    k2 = k * (1.0 + (a - 1.0) * ka_ref[...])
    kk_sq, rk_sum = gsum([kk * kk, r * k2 * rk_ref[...]])
    kk = kk * lax.rsqrt(jnp.maximum(kk_sq, 1e-24))
    bonus = rk_sum * v
    bb = kk * a
    fill(1)

    tri = tri_ref[...]
    l_hi, l_mid = _split2(logw)
    cum = _dot(tri, l_hi) + _dot(tri, l_mid)
    fill(1)
    cum_end = jnp.concatenate(
        [jnp.broadcast_to(cum[(c + 1) * t - 1:(c + 1) * t, :], (t, w))
         for c in range(tb // t)], axis=0)
    e_neg = jnp.exp(-cum)
    a_f = -kk * jnp.exp(cum - logw)
    r_f = r * jnp.exp(cum)
    a_t = a_f.astype(BF16)
    r_t = r_f.astype(BF16)
    b_h = bb * e_neg
    k_h = k2 * e_neg
    w_end = jnp.exp(cum_end)
    to_end = w_end * e_neg
    b_w = bb * to_end
    k_w = k2 * to_end
    fill(1)

    ri = lax.broadcasted_iota(jnp.int32, (t, pw), 0)
    ci = lax.broadcasted_iota(jnp.int32, (t, pw), 1)
    cm = jnp.bitwise_and(ci, t - 1)
    incl = ri >= cm
    strict = ri > cm
    eye = jnp.where(ri == cm, 1.0, 0.0)
    lane0 = ci < n
    si = lax.broadcasted_iota(jnp.int32, (pw, pw), 0)
    sj = lax.broadcasted_iota(jnp.int32, (pw, pw), 1)
    same_head = (si < n) == (sj < n)

    def bd(x):
        xb = x.astype(BF16)
        zb = jnp.zeros_like(xb)
        return jnp.concatenate([jnp.where(lane0, xb, zb), jnp.where(lane0, zb, xb)], axis=0)

    nq = RWKV_HEADS // 2
    tiles = [(c, q) for c in range(tb // t) for q in range(nq)]

    def tile(arr, c, q):
        return arr[c * t:(c + 1) * t, q * pw:(q + 1) * pw]

    sc = {}
    for cq in tiles:
        ar = jnp.concatenate([tile(a_t, *cq), tile(r_t, *cq)], axis=0)
        bk = jnp.concatenate([bd(tile(b_h, *cq)), bd(tile(k_h, *cq))], axis=0)
        sc[cq] = _dot_nt(ar, bk)
    lm, m_rb, l_ab, x = {}, {}, {}, {}
    for cq in tiles:
        l_ab[cq] = jnp.where(strict, sc[cq][0:t, 0:pw], 0.0)
        lm[cq] = jnp.concatenate(
            [jnp.where(strict, sc[cq][0:t, pw:], 0.0),
             jnp.where(incl, sc[cq][t:, pw:], 0.0)], axis=0).astype(BF16)
        m_rb[cq] = jnp.where(incl, sc[cq][t:, 0:pw], 0.0).astype(BF16)
    size = INV_BASE
    same_block = (ri // size) == (cm // size)
    powers = {}
    for cq in tiles:
        l_blk = jnp.where(same_block, l_ab[cq], 0.0)
        x[cq] = eye + l_blk
        powers[cq] = _dot(l_blk.astype(BF16), bd(l_blk))
    for _ in range(int(np.log2(size)) - 2):
        for cq in tiles:
            z = _dot(jnp.concatenate([powers[cq], x[cq]], axis=0).astype(BF16),
                     bd(powers[cq]))
            powers[cq] = z[0:t]
            x[cq] = x[cq] + z[t:]
    for cq in tiles:
        x[cq] = x[cq] + _dot(x[cq].astype(BF16), bd(powers[cq]))
    while size < t:
        join = jnp.logical_and((ri // (2 * size)) == (cm // (2 * size)),
                               (ri // size) != (cm // size))
        join0 = jnp.logical_and(join, lane0)
        join1 = jnp.logical_and(join, jnp.logical_not(lane0))
        xc = {}
        for cq in tiles:
            c_bd = jnp.concatenate([jnp.where(join0, l_ab[cq], 0.0),
                                    jnp.where(join1, l_ab[cq], 0.0)], axis=0).astype(BF16)
            xc[cq] = _dot(x[cq].astype(BF16), c_bd)
        for cq in tiles:
            x[cq] = x[cq] + _dot(xc[cq].astype(BF16), bd(x[cq]))
        size *= 2
    for cq in tiles:
        x[cq] = x[cq].astype(BF16)

    lvov, xac = {}, {}
    for cq in tiles:
        lvov[cq] = _dot(lm[cq], bd(tile(v, *cq)))
    for cq in tiles:
        rhs = jnp.concatenate([bd(tile(a_f, *cq)), bd(lvov[cq][0:t])], axis=1)
        xac[cq] = _dot(x[cq], rhs)
    mq, nn, oc, wcb = {}, {}, {}, {}
    zero_tile = jnp.zeros((t, pw), F32)
    for cq in tiles:
        bkt = jnp.transpose(
            jnp.concatenate([tile(b_w, *cq), tile(k_w, *cq)], axis=0)).astype(BF16)
        low = jnp.concatenate([zero_tile, tile(v, *cq)], axis=1)
        mn = _dot(bkt, jnp.concatenate([xac[cq], low], axis=0).astype(BF16))
        qo = _dot(m_rb[cq], jnp.concatenate(
            [bd(xac[cq][:, 0:pw]), bd(xac[cq][:, pw:])], axis=1))
        mq[cq] = jnp.concatenate(
            [jnp.where(same_head, mn[:, 0:pw], 0.0),
             tile(r_f, *cq) + qo[:, 0:pw]], axis=0).astype(BF16)
        nn[cq] = jnp.where(same_head, mn[:, pw:], 0.0)
        oc[cq] = qo[:, pw:] + lvov[cq][t:]
        c, q = cq
        wcb[cq] = jnp.transpose(jnp.broadcast_to(
            w_end[c * t:c * t + 1, q * pw:(q + 1) * pw], (pw, pw)))

    fill(1)
    h_cur = [state_ref[q] for q in range(nq)]
    for c in range(tb // t):
        for q in range(nq):
            mh = _dot(mq[c, q], h_cur[q].astype(BF16))
            y_ref[c * t:(c + 1) * t, q * pw:(q + 1) * pw] = mh[pw:] + oc[c, q]
            h_cur[q] = wcb[c, q] * h_cur[q] + mh[0:pw] + nn[c, q]
        fill(1)
    for q in range(nq):
        state_ref[q] = h_cur[q]

    fill(2)
    y = y_ref[...]
    d = y - gsum([y])[0] * (1.0 / n)
    var = gsum([d * d])[0] * (1.0 / n)
    yn = d * lax.rsqrt(var + RWKV_GN_EPS) * lnw_ref[...] + lnb_ref[...]
    return (yn + bonus) * g


def _mixer_kernel(blocks_per_row,
                  x_ref, nw_ref, win_ref, cos_ref, sin_ref, mask_ref, kte_ref, qfs_ref,
                  cd_ref, rnw_ref, ones_ref, mu_ref, w0_ref, a0_ref, wwa_ref, g2_ref,
                  kk_ref, ka_ref, rk_ref, lnw_ref, lnb_ref, gsum_ref, tri_ref,
                  o_ref, proj_a_ref, proj_b_ref, ret_state_ref, rw_state_ref, carry_ref,
                  y_ref):
    i = pl.program_id(0)
    d = RET_HEAD_DIM

    @pl.when(i == 0)
    def _():
        proj_b_ref[...] = jnp.zeros_like(proj_b_ref)

    @pl.when(jnp.logical_or(i == 0, (i + blocks_per_row - 1) % blocks_per_row == 0))
    def _():
        ret_state_ref[...] = jnp.zeros_like(ret_state_ref)
        rw_state_ref[...] = jnp.zeros_like(rw_state_ref)
        carry_ref[...] = jnp.zeros_like(carry_ref)

    def step(new_ref, mix_ref):
        hn = _rms(x_ref[...], nw_ref[...]).astype(BF16)
        cols = iter(range(0, IN_COLS, PROJ_CHUNK))

        def fill(count):
            for _ in range(count):
                lo = next(cols, None)
                if lo is not None:
                    new_ref[:, lo:lo + PROJ_CHUNK] = _dot(hn, win_ref[:, lo:lo + PROJ_CHUNK])

        y_rw = _rwkv_block(mix_ref[:, RET_COLS:], mu_ref, w0_ref, a0_ref,
                           wwa_ref, g2_ref, kk_ref, ka_ref, rk_ref, lnw_ref, lnb_ref,
                           gsum_ref, tri_ref, rw_state_ref, carry_ref, y_ref, fill)
        o_ref[:, RET_WIDTH:] = y_rw.astype(BF16)

        tiles, y_ret = _retention_block(
            lambda lo, hi: mix_ref[:, lo:hi], cos_ref[...], sin_ref[...],
            mask_ref, kte_ref, qfs_ref, cd_ref, rnw_ref, ones_ref[...], ret_state_ref, fill)
        for (ci, h), yt in zip(tiles, y_ret):
            o_ref[ci * RET_CHUNK:(ci + 1) * RET_CHUNK, h * d:(h + 1) * d] = yt.astype(BF16)
        fill(IN_COLS // PROJ_CHUNK)

    pl.when(i % 2 == 0)(functools.partial(step, proj_a_ref, proj_b_ref))
    pl.when(i % 2 == 1)(functools.partial(step, proj_b_ref, proj_a_ref))


def _mixer(x2, seq, nw, w_in_b, rnw, mu, w0, a0, wwa, g2, k_k, k_a, r_k, ln_w, ln_b):
    m = x2.shape[0]
    tb = MIX_BLOCK
    nblk = m // tb
    bpr = seq // tb
    c, d, hh = RET_CHUNK, RET_HEAD_DIM, RET_HEADS
    t, n, w = RWKV_CHUNK, RWKV_HEAD_DIM, RWKV_WIDTH
    cos, sin, mask, kte, qfs, cd = _retention_tables(seq)
    ones = jnp.full((d, d), 1.0 / d, BF16)
    grp = jnp.arange(w // 2) // n
    gmat = (grp[:, None] == grp[None, :]).astype(BF16)
    ti = jnp.arange(tb)
    tri = ((ti[:, None] >= ti[None, :])
           & (ti[:, None] // t == ti[None, :] // t)).astype(BF16)
    vec = _const_spec((1, w))
    tab_spec = pl.BlockSpec((tb, d), lambda i: ((i + bpr - 1) % bpr, 0))
    return pl.pallas_call(
        functools.partial(_mixer_kernel, bpr),
        grid=(nblk + 1,),
        in_specs=[
            pl.BlockSpec((tb, D_MODEL), lambda i: (jnp.minimum(i, nblk - 1), 0)),
            _const_spec((1, D_MODEL)),
            _const_spec((D_MODEL, IN_COLS)),
            tab_spec, tab_spec,
            _const_spec((hh, c, c)),
            _const_spec((hh, c, d)),
            _const_spec((hh, c, d)),
            _const_spec((hh, 1, d)),
            _const_spec((1, RET_WIDTH)),
            _const_spec((d, d)),
            _const_spec((1, RWKV_COLS)),
            vec, vec,
            _const_spec((DECAY_LORA + AAA_LORA, 2 * w)),
            _const_spec((GATE_LORA, w)),
            vec, vec, vec, vec, vec,
            _const_spec((w // 2, w // 2)),
            _const_spec((tb, tb)),
        ],
        out_specs=pl.BlockSpec((tb, D_MODEL), lambda i: (jnp.maximum(i - 1, 0), 0)),
        out_shape=jax.ShapeDtypeStruct((m, D_MODEL), BF16),
        scratch_shapes=[
            pltpu.VMEM((tb, IN_COLS), F32),
            pltpu.VMEM((tb, IN_COLS), F32),
            pltpu.VMEM((hh, d, d), F32),
            pltpu.VMEM((RWKV_HEADS // 2, 2 * n, 2 * n), F32),
            pltpu.VMEM((1, RWKV_COLS), F32),
            pltpu.VMEM((tb, w), F32),
        ],
        compiler_params=pltpu.CompilerParams(
            dimension_semantics=("arbitrary",), vmem_limit_bytes=V7X_VMEM_LIMIT),
        name="mixer",
    )(x2, nw, w_in_b, cos, sin, mask, kte, qfs, cd, rnw, ones, mu, w0, a0, wwa, g2,
      k_k, k_a, r_k, ln_w, ln_b, gmat, tri)


def _tail_kernel(x_ref, mixed_ref, p_ref, wo_ref, nfw_ref, wg_ref, wu_ref,
                 wd_ref, npw_ref, wpg_ref, wpu_ref, fnw_ref, o_ref):
    rows = [slice(g * TAIL_GROUP, (g + 1) * TAIL_GROUP) for g in range(TOKEN_TILE // TAIL_GROUP)]
    xs = [x_ref[r] + _dot(mixed_ref[r], wo_ref[...]) for r in rows]
    hfs = [_rms(x, nfw_ref[...]).astype(BF16) for x in xs]
    for lo in range(0, D_FF, FF_SPLIT):
        acts = []
        for hf in hfs:
            gate = _dot(hf, wg_ref[:, lo:lo + FF_SPLIT])
            up = _dot(hf, wu_ref[:, lo:lo + FF_SPLIT])
            acts.append((gate * _sigmoid(gate) * up).astype(BF16))
        xs = [x + _dot(act, wd_ref[lo:lo + FF_SPLIT, :]) for x, act in zip(xs, acts)]
    hps = [_rms(x, npw_ref[...]).astype(BF16) for x in xs]
    pgs = [_sigmoid(_dot(hp, wpg_ref[...])) for hp in hps]
    for r, x, pg in zip(rows, xs, pgs):
        x = x + _dot(p_ref[r].astype(BF16), wpu_ref[...]) * pg
        o_ref[r] = _rms(x, fnw_ref[...])


def _tail(x2, mixed, p2, w_o, nfw, w_gate, w_up, w_down, npw, w_pg, w_pu, fnw):
    m = x2.shape[0]
    tm = TOKEN_TILE
    vec = _const_spec((1, D_MODEL))
    return pl.pallas_call(
        _tail_kernel,
        grid=(m // tm,),
        in_specs=[
            pl.BlockSpec((tm, D_MODEL), lambda i: (i, 0)),
            pl.BlockSpec((tm, D_MODEL), lambda i: (i, 0)),
            pl.BlockSpec((tm, D_PLE), lambda i: (i, 0)),
            _const_spec((D_MODEL, D_MODEL)),
            vec,
            _const_spec((D_MODEL, D_FF)),
            _const_spec((D_MODEL, D_FF)),
            _const_spec((D_FF, D_MODEL)),
            vec,
            _const_spec((D_MODEL, D_MODEL)),
            _const_spec((D_PLE, D_MODEL)),
            vec,
        ],
        out_specs=pl.BlockSpec((tm, D_MODEL), lambda i: (i, 0)),
        out_shape=jax.ShapeDtypeStruct((m, D_MODEL), F32),
        compiler_params=pltpu.CompilerParams(
            dimension_semantics=("arbitrary",), vmem_limit_bytes=V7X_VMEM_LIMIT),
        name="tail",
    )(x2, mixed, p2, w_o, nfw, w_gate, w_up, w_down, npw, w_pg, w_pu, fnw)


def kernel(x, p, norm_mix_w, w_in, ret_norm_w, rw_mu, rw_w0, rw_w2, rw_a0, rw_a2,
           rw_g2, rw_k_k, rw_k_a, rw_r_k, rw_ln_w, rw_ln_b, w_o, norm_ffn_w,
           w_gate, w_up, w_down, norm_ple_w, w_ple_gate, w_ple_up, final_norm_w):
    b, s, dm = x.shape
    assert w_in.shape[0] == 1, "single-layer trunk only"
    assert s % MIX_BLOCK == 0 and (b * s) % TOKEN_TILE == 0
    x2 = x.reshape(b * s, dm)
    row = lambda a: a.reshape(1, -1)
    w_in_b = w_in[0].astype(BF16)
    w_in_b = jnp.concatenate([_even_odd_columns(w_in_b[:, :2 * RET_WIDTH]),
                              w_in_b[:, 2 * RET_WIDTH:]], axis=1)
    zeros = jnp.zeros((DECAY_LORA, RWKV_WIDTH), F32)
    wwa = jnp.concatenate([
        jnp.concatenate([rw_w2[0], zeros], axis=1),
        jnp.concatenate([zeros, rw_a2[0]], axis=1)], axis=0).astype(BF16)
    mixed = _mixer(x2, s, row(norm_mix_w[0]), w_in_b, row(ret_norm_w[0]), row(rw_mu[0]),
                   row(rw_w0[0]), row(rw_a0[0]), wwa, rw_g2[0].astype(BF16),
                   row(rw_k_k[0]), row(rw_k_a[0]), row(rw_r_k[0]), row(rw_ln_w[0]),
                   row(rw_ln_b[0]))
    out = _tail(x2, mixed, p[0].reshape(b * s, D_PLE), w_o[0].astype(BF16),
                row(norm_ffn_w[0]), w_gate[0].astype(BF16), w_up[0].astype(BF16),
                w_down[0].astype(BF16), row(norm_ple_w[0]), w_ple_gate[0].astype(BF16),
                w_ple_up[0].astype(BF16), row(final_norm_w))
    return out.reshape(b, s, dm)
```
